```python
import math
import jax, jax.numpy as jnp
from jax import lax
import numpy as np

D_MODEL = 1024
BATCH = 16
SEQ = 2048
DEPTH = 1
DEC_BATCH = 128
DEC_SEQ = 1
PAST_LEN = 16384
PAGE_SIZE = 128

N_META = 16
N_HEADS = 8
Q_RANK = 256
KV_RANK = 256
NOPE_DIM = 64
ROPE_DIM = 32
QK_DIM = NOPE_DIM + ROPE_DIM
V_DIM = D_MODEL // N_HEADS
ROPE_BASE = 10000.0
ATTN_SCALE = 1.0 / math.sqrt(QK_DIM)
POOL_WIDTH = D_MODEL // 2
POOL_WINDOWS = (2, 4, 8, 16)
N_POOL_GROUPS = len(POOL_WINDOWS)
POOL_GROUP = POOL_WIDTH // N_POOL_GROUPS
POOL_OUT_GROUP = D_MODEL // N_POOL_GROUPS
POOL_HIST = max(POOL_WINDOWS) - 1
N_EXPERTS = 32
TOP_K = 4
D_FF = D_MODEL
SWIGLU_LIMIT = 7.0
SWIGLU_ALPHA = 1.702
MOE_BLOCK = 128
Q_BLOCK = 128
EPS = 1e-6
IN_COLS = Q_RANK + KV_RANK + ROPE_DIM + POOL_WIDTH + 2 * D_MODEL

kernel_name = 'mla_pool_moe_hybrid_step'


def rmsnorm(x, g):
    xf = x.astype(jnp.float32)
    y = xf * lax.rsqrt(jnp.mean(xf * xf, axis=-1, keepdims=True) + EPS)
    return (y * g.astype(jnp.float32)).astype(x.dtype)


def rope(x, pos):
    half = ROPE_DIM // 2
    inv = ROPE_BASE ** (-jnp.arange(half, dtype=jnp.float32) / half)
    ang = pos.astype(jnp.float32)[:, None] * inv[None, :]
    shape = (1, ang.shape[0]) + (1,) * (x.ndim - 3) + (half,)
    cos = jnp.cos(ang).reshape(shape)
    sin = jnp.sin(ang).reshape(shape)
    xf = x.astype(jnp.float32)
    x1, x2 = xf[..., :half], xf[..., half:]
    return jnp.concatenate([x1 * cos - x2 * sin, x1 * sin + x2 * cos], axis=-1).astype(x.dtype)


def split_in(z):
    i0 = Q_RANK
    i1 = i0 + KV_RANK
    i2 = i1 + ROPE_DIM
    i3 = i2 + POOL_WIDTH
    i4 = i3 + D_MODEL
    return z[..., :i0], z[..., i0:i1], z[..., i1:i2], z[..., i2:i3], z[..., i3:i4], z[..., i4:]


def mla_attend_prompt(q_lat, q_rope, ckv, kr):
    B, L = q_lat.shape[0], q_lat.shape[1]
    n_blk = -(-L // Q_BLOCK)
    Lp = n_blk * Q_BLOCK
    pad = Lp - L
    q_lat = jnp.pad(q_lat, ((0, 0), (0, pad), (0, 0), (0, 0)))
    q_rope = jnp.pad(q_rope, ((0, 0), (0, pad), (0, 0), (0, 0)))
    ckv = jnp.pad(ckv, ((0, 0), (0, pad), (0, 0)))
    kr = jnp.pad(kr, ((0, 0), (0, pad), (0, 0)))
    qb = jnp.moveaxis(q_lat.reshape(B, n_blk, Q_BLOCK, N_HEADS, KV_RANK), 1, 0)
    rb = jnp.moveaxis(q_rope.reshape(B, n_blk, Q_BLOCK, N_HEADS, ROPE_DIM), 1, 0)
    kpos = jnp.arange(Lp)

    def block(args):
        ql, qr, i = args
        qpos = i * Q_BLOCK + jnp.arange(Q_BLOCK)
        s = (jnp.einsum('bqhc,bkc->bhqk', ql, ckv, preferred_element_type=jnp.float32)
             + jnp.einsum('bqhr,bkr->bhqk', qr, kr, preferred_element_type=jnp.float32)) * ATTN_SCALE
        s = jnp.where(kpos[None, None, None, :] <= qpos[None, None, :, None], s, -jnp.inf)
        p = jax.nn.softmax(s, axis=-1).astype(ckv.dtype)
        return jnp.einsum('bhqk,bkc->bqhc', p, ckv)

    o = lax.map(block, (qb, rb, jnp.arange(n_blk)))
    return jnp.moveaxis(o, 0, 1).reshape(B, Lp, N_HEADS, KV_RANK)[:, :L]


def mla_attend_sample(q_lat, q_rope, ckv_new, kr_new, ckv_past, kr_past):
    S = q_lat.shape[1]
    P = ckv_past.shape[1]
    s_past = (jnp.einsum('bqhc,bkc->bhqk', q_lat, ckv_past, preferred_element_type=jnp.float32)
              + jnp.einsum('bqhr,bkr->bhqk', q_rope, kr_past, preferred_element_type=jnp.float32))
    s_new = (jnp.einsum('bqhc,bkc->bhqk', q_lat, ckv_new, preferred_element_type=jnp.float32)
             + jnp.einsum('bqhr,bkr->bhqk', q_rope, kr_new, preferred_element_type=jnp.float32))
    causal = jnp.arange(S)[None, :] <= jnp.arange(S)[:, None]
    s_new = jnp.where(causal[None, None], s_new, -jnp.inf)
    s = jnp.concatenate([s_past, s_new], axis=-1) * ATTN_SCALE
    p = jax.nn.softmax(s, axis=-1).astype(ckv_new.dtype)
    return (jnp.einsum('bhqk,bkc->bqhc', p[..., :P], ckv_past)
            + jnp.einsum('bhqk,bkc->bqhc', p[..., P:], ckv_new))


def pool_mix(p_ext, pos, w_pool, pool_scale):
    B = p_ext.shape[0]
    L = p_ext.shape[1] - POOL_HIST
    pf = p_ext.astype(jnp.float32)
    cs = jnp.cumsum(pf, axis=1)
    cs = jnp.concatenate([jnp.zeros_like(cs[:, :1]), cs], axis=1)
    outs = []
    for g, w in enumerate(POOL_WINDOWS):
        lo, hi = g * POOL_GROUP, (g + 1) * POOL_GROUP
        s = cs[:, POOL_HIST + 1:POOL_HIST + 1 + L, lo:hi] - cs[:, POOL_HIST + 1 - w:POOL_HIST + 1 - w + L, lo:hi]
        cnt = jnp.minimum(w, pos + 1).astype(jnp.float32)
        outs.append(s / cnt[None, :, None])
    pooled = jnp.concatenate(outs, axis=-1) - pf[:, POOL_HIST:]
    grouped = pooled.reshape(B, L, N_POOL_GROUPS, POOL_GROUP)
    y = jnp.einsum('blgc,gco->blgo', grouped, w_pool.astype(jnp.float32)).reshape(B, L, D_MODEL)
    return (y * pool_scale.astype(jnp.float32)).astype(p_ext.dtype)


def moe(h, w_router, b_router, w_gate_up, b_gate_up, w_down, b_down):
    B, L, D = h.shape
    t = h.reshape(-1, D)
    T = t.shape[0]
    logits = (t @ w_router + b_router).astype(jnp.float32)
    top_v, top_i = lax.top_k(logits, TOP_K)
    gates = jax.nn.softmax(top_v, axis=-1)
    n_asg = T * TOP_K
    e = top_i.reshape(-1)
    tok = jnp.repeat(jnp.arange(T), TOP_K)
    g = gates.reshape(-1)
    order = jnp.argsort(e)
    e_s, tok_s, g_s = e[order], tok[order], g[order]
    counts = jnp.bincount(e, length=N_EXPERTS)
    padded = (counts + MOE_BLOCK - 1) // MOE_BLOCK * MOE_BLOCK
    start = jnp.cumsum(counts) - counts
    pend = jnp.cumsum(padded)
    pstart = pend - padded
    dest = pstart[e_s] + jnp.arange(n_asg) - start[e_s]
    n_blocks = -(-n_asg // MOE_BLOCK) + N_EXPERTS
    cap = n_blocks * MOE_BLOCK
    xp = jnp.zeros((cap, D), t.dtype).at[dest].set(t[tok_s])
    blk_e = jnp.minimum(jnp.searchsorted(pend, jnp.arange(n_blocks) * MOE_BLOCK, side='right'), N_EXPERTS - 1)

    def expert_block(args):
        xb, ei = args
        gu = xb @ w_gate_up[ei] + b_gate_up[ei]
        gate = jnp.minimum(gu[:, :D_FF], SWIGLU_LIMIT)
        up = jnp.clip(gu[:, D_FF:], -SWIGLU_LIMIT, SWIGLU_LIMIT)
        act = gate * jax.nn.sigmoid(gate * SWIGLU_ALPHA)
        return ((up + 1.0) * act) @ w_down[ei] + b_down[ei]

    yp = lax.map(expert_block, (xp.reshape(n_blocks, MOE_BLOCK, D), blk_e)).reshape(cap, D)
    y = jax.ops.segment_sum(yp[dest] * g_s[:, None].astype(yp.dtype), tok_s, num_segments=T)
    return y.reshape(B, L, D)


def decoder_layer(x, pos, pool_hist, past, norm1_g, w_in, q_norm_g, w_uq, kv_norm_g, w_uk, w_uv,
                  w_pool, pool_scale, w_out, norm2_g, w_router, b_router, w_gate_up, b_gate_up,
                  w_down, b_down):
    B, L, _ = x.shape
    h = rmsnorm(x, norm1_g)
    cq, ckv, kr, p, ga, gb = split_in(h @ w_in)
    q = (rmsnorm(cq, q_norm_g) @ w_uq).reshape(B, L, N_HEADS, QK_DIM)
    q_rope = rope(q[..., NOPE_DIM:], pos)
    q_lat = jnp.einsum('blhn,hnc->blhc', q[..., :NOPE_DIM], w_uk)
    ckv = rmsnorm(ckv, kv_norm_g)
    kr = rope(kr, pos)
    if past is None:
        o_lat = mla_attend_prompt(q_lat, q_rope, ckv, kr)
    else:
        o_lat = mla_attend_sample(q_lat, q_rope, ckv, kr, past[0], past[1])
    a = jnp.einsum('blhc,hcv->blhv', o_lat, w_uv).reshape(B, L, D_MODEL)
    p_ext = jnp.concatenate([pool_hist.astype(p.dtype), p], axis=1)
    b = pool_mix(p_ext, pos, w_pool, pool_scale)
    merged = jax.nn.sigmoid(ga) * a + jax.nn.sigmoid(gb) * b
    x = x + merged @ w_out
    x = x + moe(rmsnorm(x, norm2_g), w_router, b_router, w_gate_up, b_gate_up, w_down, b_down)
    return x, ckv, kr, p_ext[:, -POOL_HIST:]


def setup_inputs(seed: int = 0) -> dict:
    key = jax.random.key(seed)
    ks = jax.random.split(key, 32)
    f32 = jnp.float32
    n_pages = PAST_LEN // PAGE_SIZE
    n_phys = (DEC_BATCH * n_pages * 5) // 4

    def nrm(k, shape, scale):
        return jax.random.normal(k, shape, f32) * scale

    def gain(k, shape):
        return 1.0 + 0.05 * jax.random.normal(k, shape, f32)

    page_table = jax.random.permutation(ks[5], n_phys)[:DEC_BATCH * n_pages].reshape(DEC_BATCH, n_pages).astype(jnp.int32)
    return {
        'x_prompt': nrm(ks[0], (BATCH, SEQ, D_MODEL), 1.0),
        'x_sample': nrm(ks[1], (DEC_BATCH, DEC_SEQ, D_MODEL), 1.0),
        'cache_ckv': nrm(ks[2], (DEPTH, n_phys, PAGE_SIZE, KV_RANK), 1.0),
        'cache_krope': nrm(ks[3], (DEPTH, n_phys, PAGE_SIZE, ROPE_DIM), 1.0),
        'state_pool': nrm(ks[4], (DEPTH, DEC_BATCH, POOL_HIST, POOL_WIDTH), 1.0),
        'page_table': page_table,
        'meta_tokens': nrm(ks[6], (N_META, D_MODEL), 1.0),
        'norm1_g': gain(ks[7], (DEPTH, D_MODEL)),
        'w_in': nrm(ks[8], (DEPTH, D_MODEL, IN_COLS), D_MODEL ** -0.5),
        'q_norm_g': gain(ks[9], (DEPTH, Q_RANK)),
        'w_uq': nrm(ks[10], (DEPTH, Q_RANK, N_HEADS * QK_DIM), Q_RANK ** -0.5),
        'kv_norm_g': gain(ks[11], (DEPTH, KV_RANK)),
        'w_uk': nrm(ks[12], (DEPTH, N_HEADS, NOPE_DIM, KV_RANK), KV_RANK ** -0.5),
        'w_uv': nrm(ks[13], (DEPTH, N_HEADS, KV_RANK, V_DIM), KV_RANK ** -0.5),
        'w_pool': nrm(ks[14], (DEPTH, N_POOL_GROUPS, POOL_GROUP, POOL_OUT_GROUP), POOL_GROUP ** -0.5),
        'pool_scale': gain(ks[15], (DEPTH, D_MODEL)),
        'w_out': nrm(ks[16], (DEPTH, D_MODEL, D_MODEL), D_MODEL ** -0.5),
        'norm2_g': gain(ks[17], (DEPTH, D_MODEL)),
        'w_router': nrm(ks[18], (DEPTH, D_MODEL, N_EXPERTS), D_MODEL ** -0.5),
        'b_router': nrm(ks[19], (DEPTH, N_EXPERTS), 0.01),
        'w_gate_up': nrm(ks[20], (DEPTH, N_EXPERTS, D_MODEL, 2 * D_FF), D_MODEL ** -0.5),
        'b_gate_up': nrm(ks[21], (DEPTH, N_EXPERTS, 2 * D_FF), 0.01),
        'w_down': nrm(ks[22], (DEPTH, N_EXPERTS, D_FF, D_MODEL), D_FF ** -0.5),
        'b_down': nrm(ks[23], (DEPTH, N_EXPERTS, D_MODEL), 0.01),
        'final_norm_g': gain(ks[24], (D_MODEL,)),
    }


def reference(x_prompt, x_sample, cache_ckv, cache_krope, state_pool, page_table, meta_tokens,
              norm1_g, w_in, q_norm_g, w_uq, kv_norm_g, w_uk, w_uv, w_pool, pool_scale, w_out,
              norm2_g, w_router, b_router, w_gate_up, b_gate_up, w_down, b_down, final_norm_g):
    B = x_prompt.shape[0]
    Bd = x_sample.shape[0]
    n_pages = PAST_LEN // PAGE_SIZE
    xp = jnp.concatenate([jnp.broadcast_to(meta_tokens.astype(x_prompt.dtype)[None], (B, N_META, D_MODEL)), x_prompt], axis=1)
    pos_p = jnp.arange(xp.shape[1])
    pos_s = PAST_LEN + jnp.arange(x_sample.shape[1])
    xs = x_sample
    ckv_p_l, kr_p_l, pool_p_l, ckv_s_l, kr_s_l, pool_s_l = [], [], [], [], [], []
    for l in range(DEPTH):
        lw = (norm1_g[l], w_in[l], q_norm_g[l], w_uq[l], kv_norm_g[l], w_uk[l], w_uv[l], w_pool[l],
              pool_scale[l], w_out[l], norm2_g[l], w_router[l], b_router[l], w_gate_up[l],
              b_gate_up[l], w_down[l], b_down[l])
        hist0 = jnp.zeros((B, POOL_HIST, POOL_WIDTH), xp.dtype)
        xp, ckv_p, kr_p, pool_p = decoder_layer(xp, pos_p, hist0, None, *lw)
        past = (cache_ckv[l][page_table].reshape(Bd, n_pages * PAGE_SIZE, KV_RANK),
                cache_krope[l][page_table].reshape(Bd, n_pages * PAGE_SIZE, ROPE_DIM))
        xs, ckv_s, kr_s, pool_s = decoder_layer(xs, pos_s, state_pool[l], past, *lw)
        ckv_p_l.append(ckv_p); kr_p_l.append(kr_p); pool_p_l.append(pool_p)
        ckv_s_l.append(ckv_s); kr_s_l.append(kr_s); pool_s_l.append(pool_s)
    y_prompt = rmsnorm(xp, final_norm_g)[:, N_META:]
    y_sample = rmsnorm(xs, final_norm_g)
    new_ckv_prompt = jnp.stack(ckv_p_l)
    new_krope_prompt = jnp.stack(kr_p_l)
    new_pool_prompt = jnp.stack(pool_p_l)
    new_ckv_sample = jnp.stack(ckv_s_l)
    new_krope_sample = jnp.stack(kr_s_l)
    new_pool_sample = jnp.stack(pool_s_l)
    return (y_prompt, y_sample, new_ckv_prompt, new_krope_prompt, new_pool_prompt, new_ckv_sample, new_krope_sample, new_pool_sample)
```

```python
import functools
import math

import jax
import jax.numpy as jnp
from jax import lax
from jax.experimental import pallas as pl
from jax.experimental.pallas import tpu as pltpu

F32 = jnp.float32
BF16 = jnp.bfloat16

D_MODEL = 1024
N_META = 16
N_HEADS = 8
Q_RANK = 256
KV_RANK = 256
NOPE_DIM = 64
ROPE_DIM = 32
ROPE_HALF = ROPE_DIM // 2
QK_DIM = NOPE_DIM + ROPE_DIM
V_DIM = D_MODEL // N_HEADS
ROPE_BASE = 10000.0
ATTN_SCALE = 1.0 / math.sqrt(QK_DIM)
POOL_WIDTH = D_MODEL // 2
POOL_WINDOWS = (2, 4, 8, 16)
POOL_GROUP = POOL_WIDTH // len(POOL_WINDOWS)
POOL_OUT_GROUP = D_MODEL // len(POOL_WINDOWS)
POOL_HIST = max(POOL_WINDOWS) - 1
N_EXPERTS = 32
TOP_K = 4
D_FF = D_MODEL
SWIGLU_LIMIT = 7.0
SWIGLU_ALPHA = 1.702
EPS = 1e-6
PAGE_SIZE = 128

LANES = 128
NEG_BIG = -1e30

C_CQ = 0
C_CKV = C_CQ + Q_RANK
C_K1 = C_CKV + KV_RANK
C_K2 = C_K1 + LANES
C_P = C_K2 + LANES
C_GA = C_P + POOL_WIDTH
C_GB = C_GA + D_MODEL
C_END = C_GB + D_MODEL
Q_NOPE_W = N_HEADS * NOPE_DIM
Q_COLS = Q_NOPE_W + 2 * LANES
KCAT_W = KV_RANK + 2 * LANES

FRONT_TM = 256
ATTN_T = 256
POST_TM = 256
MOE_TM = 256
SATTN_PAGES = 16
VMEM_LIMIT = 56 * 1024 * 1024


def _rms(x, g):
    return x * lax.rsqrt(jnp.mean(x * x, axis=-1, keepdims=True) + EPS) * g


def _sigmoid(x):
    return 1.0 / (1.0 + jnp.exp(-x))


def _rope_pair_to_rows(k1r, k2r):
    lane = lax.broadcasted_iota(jnp.int32, k1r.shape, 1)
    return jnp.where(lane < ROPE_HALF, k1r, pltpu.roll(k2r, ROPE_HALF, axis=1))[:, 0:ROPE_DIM]


def _front_kernel(x_ref, g1_ref, w_ref, qg_ref, wuq_ref, kvg_ref, cos_ref, sin_ref, wpool_ref,
                  pscale_ref, pprev_ref,
                  q_ref, kcat_ref, ckv_ref, kr_ref, ptail_ref, sga_ref, gbb_ref,
                  pext_ref, *, tm):
    i = pl.program_id(1)
    x = x_ref[0]
    hb = _rms(x, g1_ref[...]).astype(BF16)
    z = jnp.dot(hb, w_ref[...], preferred_element_type=F32)
    cos = cos_ref[...]
    sin = sin_ref[...]

    qn = _rms(z[:, C_CQ:C_CQ + Q_RANK], qg_ref[...]).astype(BF16)
    q = jnp.dot(qn, wuq_ref[...], preferred_element_type=F32) * ATTN_SCALE
    q1 = q[:, Q_NOPE_W:Q_NOPE_W + LANES]
    q2 = q[:, Q_NOPE_W + LANES:]
    q_ref[0, :, 0:Q_NOPE_W] = q[:, 0:Q_NOPE_W].astype(BF16)
    q_ref[0, :, Q_NOPE_W:Q_NOPE_W + LANES] = (q1 * cos - q2 * sin).astype(BF16)
    q_ref[0, :, Q_NOPE_W + LANES:] = (q1 * sin + q2 * cos).astype(BF16)

    ckv = _rms(z[:, C_CKV:C_CKV + KV_RANK], kvg_ref[...])
    ckv_ref[0] = ckv
    k1 = z[:, C_K1:C_K1 + LANES]
    k2 = z[:, C_K2:C_K2 + LANES]
    k1r = k1 * cos - k2 * sin
    k2r = k1 * sin + k2 * cos
    kcat_ref[0, :, 0:KV_RANK] = ckv.astype(BF16)
    kcat_ref[0, :, KV_RANK:KV_RANK + LANES] = k1r.astype(BF16)
    kcat_ref[0, :, KV_RANK + LANES:] = k2r.astype(BF16)
    kr_ref[0] = _rope_pair_to_rows(k1r, k2r)

    p = z[:, C_P:C_P + POOL_WIDTH]

    @pl.when(i == 0)
    def _():
        pext_ref[0:16, :] = pprev_ref[...]

    @pl.when(i > 0)
    def _():
        pext_ref[0:16, :] = pext_ref[tm:tm + 16, :]

    pext_ref[16:16 + tm, :] = p
    ptail_ref[0] = p[tm - 16:tm, :]
    bs = []
    for g, w in enumerate(POOL_WINDOWS):
        lo = g * POOL_GROUP
        acc = pext_ref[16:16 + tm, lo:lo + POOL_GROUP]
        for k in range(1, w):
            acc = acc + pext_ref[16 - k:16 - k + tm, lo:lo + POOL_GROUP]
        pooled = acc * (1.0 / w) - p[:, lo:lo + POOL_GROUP]
        bs.append(jnp.dot(pooled.astype(BF16), wpool_ref[g], preferred_element_type=F32))
    b = jnp.concatenate(bs, axis=1) * pscale_ref[...]
    sga_ref[0] = _sigmoid(z[:, C_GA:C_GA + D_MODEL]).astype(BF16)
    gbb_ref[0] = (_sigmoid(z[:, C_GB:C_GB + D_MODEL]) * b).astype(BF16)


def _front(x, cos_t, sin_t, pprev, wts, tm):
    B, L, _ = x.shape
    nt = L // tm
    full = lambda shape: pl.BlockSpec(shape, lambda b, i: (0,) * len(shape))
    row = lambda w: pl.BlockSpec((1, tm, w), lambda b, i: (b, i, 0))
    out_shape = (
        jax.ShapeDtypeStruct((B, L, Q_COLS), BF16),
        jax.ShapeDtypeStruct((B, L, KCAT_W), BF16),
        jax.ShapeDtypeStruct((B, L, KV_RANK), F32),
        jax.ShapeDtypeStruct((B, L, ROPE_DIM), F32),
        jax.ShapeDtypeStruct((B, 16, POOL_WIDTH), F32),
        jax.ShapeDtypeStruct((B, L, D_MODEL), BF16),
        jax.ShapeDtypeStruct((B, L, D_MODEL), BF16),
    )
    return pl.pallas_call(
        functools.partial(_front_kernel, tm=tm),
        out_shape=out_shape,
        grid=(B, nt),
        in_specs=[
            row(D_MODEL),
            full((1, D_MODEL)),
            full((D_MODEL, C_END)),
            full((1, Q_RANK)),
            full((Q_RANK, Q_COLS)),
            full((1, KV_RANK)),
            pl.BlockSpec((tm, LANES), lambda b, i: (i, 0)),
            pl.BlockSpec((tm, LANES), lambda b, i: (i, 0)),
            full((len(POOL_WINDOWS), POOL_GROUP, POOL_OUT_GROUP)),
            full((1, D_MODEL)),
            full((16, POOL_WIDTH)),
        ],
        out_specs=(
            row(Q_COLS), row(KCAT_W), row(KV_RANK), row(ROPE_DIM),
            pl.BlockSpec((1, 16, POOL_WIDTH), lambda b, i: (b, 0, 0)),
            row(D_MODEL), row(D_MODEL),
        ),
        scratch_shapes=[pltpu.VMEM((tm + 16, POOL_WIDTH), F32)],
        compiler_params=pltpu.CompilerParams(
            dimension_semantics=("arbitrary", "arbitrary"), vmem_limit_bytes=VMEM_LIMIT),
        name="front",
    )(x, wts["g1"], wts["w_in"], wts["qg"], wts["w_uq"], wts["kvg"], cos_t, sin_t, wts["w_pool"],
      wts["pool_scale"], pprev)


def _build_qcat(q, wukp_ref, qcat_ref, t):
    lane = lax.broadcasted_iota(jnp.int32, (t, LANES), 1)
    q1 = q[:, Q_NOPE_W:Q_NOPE_W + LANES].astype(F32)
    q2 = q[:, Q_NOPE_W + LANES:].astype(F32)
    zero = jnp.zeros_like(q1)
    for jp in range(N_HEADS // 2):
        ql = jnp.dot(q[:, jp * LANES:(jp + 1) * LANES], wukp_ref[jp], preferred_element_type=F32)
        for s in range(2):
            h = 2 * jp + s
            sel = (lane >= h * ROPE_HALF) & (lane < (h + 1) * ROPE_HALF)
            qcat_ref[h * t:(h + 1) * t, 0:KV_RANK] = ql[:, s * KV_RANK:(s + 1) * KV_RANK].astype(BF16)
            qcat_ref[h * t:(h + 1) * t, KV_RANK:KV_RANK + LANES] = jnp.where(sel, q1, zero).astype(BF16)
            qcat_ref[h * t:(h + 1) * t, KV_RANK + LANES:] = jnp.where(sel, q2, zero).astype(BF16)


def _attn_kernel(qi_ref, kj_ref, q_ref, kcat_ref, kmeta_ref, wukp_ref, wuv_ref, sga_ref, gbb_ref,
                 out_ref, qcat_ref, m_ref, l_ref, acc_ref, *, t):
    s_id = pl.program_id(1)
    i = qi_ref[s_id]
    j = kj_ref[s_id]
    rows = N_HEADS * t
    nt_dims = (((1,), (1,)), ((), ()))

    @pl.when(j == 0)
    def _():
        _build_qcat(q_ref[0], wukp_ref, qcat_ref, t)
        km = kmeta_ref[...]
        s = lax.dot_general(qcat_ref[...], km, nt_dims, preferred_element_type=F32)
        col = lax.broadcasted_iota(jnp.int32, (rows, LANES), 1)
        s = jnp.where(col < N_META, s, NEG_BIG)
        m = jnp.max(s, axis=1, keepdims=True)
        p = jnp.exp(s - m)
        m_ref[...] = jnp.broadcast_to(m, (rows, LANES))
        l_ref[...] = jnp.broadcast_to(jnp.sum(p, axis=1, keepdims=True), (rows, LANES))
        acc_ref[...] = jnp.dot(p.astype(BF16), km[:, 0:KV_RANK], preferred_element_type=F32)

    k = kcat_ref[0]
    s = lax.dot_general(qcat_ref[...], k, nt_dims, preferred_element_type=F32)
    r_tok = lax.broadcasted_iota(jnp.int32, (rows, t), 0) & (t - 1)
    col = lax.broadcasted_iota(jnp.int32, (rows, t), 1)
    s = jnp.where(col - r_tok <= (i - j) * t, s, NEG_BIG)
    m_prev = m_ref[...]
    m_new = jnp.maximum(m_prev, jnp.max(s, axis=1, keepdims=True))
    alpha = jnp.exp(m_prev - m_new)
    p = jnp.exp(s - jnp.concatenate([m_new] * (t // LANES), axis=1))
    l_ref[...] = alpha * l_ref[...] + jnp.sum(p, axis=1, keepdims=True)
    m_ref[...] = m_new
    pv = jnp.dot(p.astype(BF16), k[:, 0:KV_RANK], preferred_element_type=F32)
    acc_ref[...] = acc_ref[...] * jnp.concatenate([alpha] * (KV_RANK // LANES), axis=1) + pv

    @pl.when(j == i)
    def _():
        inv = 1.0 / l_ref[...]
        o = acc_ref[...] * jnp.concatenate([inv] * (KV_RANK // LANES), axis=1)
        a = jnp.concatenate(
            [jnp.dot(o[h * t:(h + 1) * t].astype(BF16), wuv_ref[h], preferred_element_type=F32)
             for h in range(N_HEADS)], axis=1)
        merged = sga_ref[0].astype(F32) * a + gbb_ref[0].astype(F32)
        out_ref[0] = merged.astype(BF16)


def _attn(q, kcat, kmeta, sga, gbb, wts, t):
    B, L, _ = q.shape
    nq = L // t
    pairs = [(i, j) for i in range(nq) for j in range(i + 1)]
    qi = jnp.asarray([p[0] for p in pairs], jnp.int32)
    kj = jnp.asarray([p[1] for p in pairs], jnp.int32)
    rows = N_HEADS * t
    grid_spec = pltpu.PrefetchScalarGridSpec(
        num_scalar_prefetch=2,
        grid=(B, len(pairs)),
        in_specs=[
            pl.BlockSpec((1, t, Q_COLS), lambda b, s, qi, kj: (b, qi[s], 0)),
            pl.BlockSpec((1, t, KCAT_W), lambda b, s, qi, kj: (b, kj[s], 0)),
            pl.BlockSpec((LANES, KCAT_W), lambda b, s, qi, kj: (0, 0)),
            pl.BlockSpec((N_HEADS // 2, LANES, 2 * KV_RANK), lambda b, s, qi, kj: (0, 0, 0)),
            pl.BlockSpec((N_HEADS, KV_RANK, V_DIM), lambda b, s, qi, kj: (0, 0, 0)),
            pl.BlockSpec((1, t, D_MODEL), lambda b, s, qi, kj: (b, qi[s], 0)),
            pl.BlockSpec((1, t, D_MODEL), lambda b, s, qi, kj: (b, qi[s], 0)),
        ],
        out_specs=pl.BlockSpec((1, t, D_MODEL), lambda b, s, qi, kj: (b, qi[s], 0)),
        scratch_shapes=[
            pltpu.VMEM((rows, KCAT_W), BF16),
            pltpu.VMEM((rows, LANES), F32),
            pltpu.VMEM((rows, LANES), F32),
            pltpu.VMEM((rows, KV_RANK), F32),
        ],
    )
    return pl.pallas_call(
        functools.partial(_attn_kernel, t=t),
        out_shape=jax.ShapeDtypeStruct((B, L, D_MODEL), BF16),
        grid_spec=grid_spec,
        compiler_params=pltpu.CompilerParams(
            dimension_semantics=("arbitrary", "arbitrary"), vmem_limit_bytes=VMEM_LIMIT),
        name="attn",
    )(qi, kj, q, kcat, kmeta, wts["w_ukp"], wts["w_uv"], sga, gbb)


def _sfront_kernel(x_ref, g1_ref, w_ref, qg_ref, wuq_ref, kvg_ref, cos_ref, sin_ref, wpool_ref,
                   pscale_ref, hist_ref, wukp_ref,
                   qlat_ref, q1_ref, q2_ref, ckv_ref, kr_ref, p_ref, sga_ref, gbb_ref):
    n = x_ref.shape[0]
    x = x_ref[...]
    hb = _rms(x, g1_ref[...]).astype(BF16)
    z = jnp.dot(hb, w_ref[...], preferred_element_type=F32)
    cos = cos_ref[...]
    sin = sin_ref[...]

    qn = _rms(z[:, C_CQ:C_CQ + Q_RANK], qg_ref[...]).astype(BF16)
    q = jnp.dot(qn, wuq_ref[...], preferred_element_type=F32) * ATTN_SCALE
    q1 = q[:, Q_NOPE_W:Q_NOPE_W + LANES]
    q2 = q[:, Q_NOPE_W + LANES:]
    q1_ref[...] = q1 * cos - q2 * sin
    q2_ref[...] = q1 * sin + q2 * cos
    qb = q[:, 0:Q_NOPE_W].astype(BF16)
    for jp in range(N_HEADS // 2):
        ql = jnp.dot(qb[:, jp * LANES:(jp + 1) * LANES], wukp_ref[jp], preferred_element_type=F32)
        qlat_ref[2 * jp] = ql[:, 0:KV_RANK].astype(BF16)
        qlat_ref[2 * jp + 1] = ql[:, KV_RANK:].astype(BF16)

    ckv = _rms(z[:, C_CKV:C_CKV + KV_RANK], kvg_ref[...])
    ckv_ref[...] = ckv
    k1 = z[:, C_K1:C_K1 + LANES]
    k2 = z[:, C_K2:C_K2 + LANES]
    k1r = k1 * cos - k2 * sin
    k2r = k1 * sin + k2 * cos
    kr_ref[...] = _rope_pair_to_rows(k1r, k2r)

    p = z[:, C_P:C_P + POOL_WIDTH]
    p_ref[...] = p
    bs = []
    for g, w in enumerate(POOL_WINDOWS):
        lo = g * POOL_GROUP
        acc = p[:, lo:lo + POOL_GROUP]
        for k in range(1, w):
            acc = acc + hist_ref[POOL_HIST - k, :, lo:lo + POOL_GROUP]
        pooled = acc * (1.0 / w) - p[:, lo:lo + POOL_GROUP]
        bs.append(jnp.dot(pooled.astype(BF16), wpool_ref[g], preferred_element_type=F32))
    b = jnp.concatenate(bs, axis=1) * pscale_ref[...]
    sga_ref[...] = _sigmoid(z[:, C_GA:C_GA + D_MODEL]).astype(BF16)
    gbb_ref[...] = (_sigmoid(z[:, C_GB:C_GB + D_MODEL]) * b).astype(BF16)


def _sfront(xs, cos_s, sin_s, hist_t, wts):
    n = xs.shape[0]
    out_shape = (
        jax.ShapeDtypeStruct((N_HEADS, n, KV_RANK), BF16),
        jax.ShapeDtypeStruct((n, LANES), F32),
        jax.ShapeDtypeStruct((n, LANES), F32),
        jax.ShapeDtypeStruct((n, KV_RANK), F32),
        jax.ShapeDtypeStruct((n, ROPE_DIM), F32),
        jax.ShapeDtypeStruct((n, POOL_WIDTH), F32),
        jax.ShapeDtypeStruct((n, D_MODEL), BF16),
        jax.ShapeDtypeStruct((n, D_MODEL), BF16),
    )
    return pl.pallas_call(
        _sfront_kernel,
        out_shape=out_shape,
        compiler_params=pltpu.CompilerParams(vmem_limit_bytes=VMEM_LIMIT),
        name="sfront",
    )(xs, wts["g1"], wts["w_in"], wts["qg"], wts["w_uq"], wts["kvg"], cos_s, sin_s, wts["w_pool"],
      wts["pool_scale"], hist_t, wts["w_ukp"])


def _sattn_kernel(pt_ref, qlat_ref, qrope_ref, ckvn_ref, krn_ref, cckv_hbm, ckr_hbm,
                  o_ref, kbuf, rbuf, sem, m_ref, l_ref, acc_ref, *, pc, nc):
    b = pl.program_id(0)
    c = pl.program_id(1)
    nb = pl.num_programs(0)
    step = b * nc + c
    slot = step % 2
    nt_dims = (((1,), (1,)), ((), ()))

    def page_copies(bb, cc, sl):
        cps = []
        for pg in range(pc):
            phys = pt_ref[bb * (nc * pc) + cc * pc + pg]
            cps.append(pltpu.make_async_copy(
                cckv_hbm.at[phys], kbuf.at[sl, pl.ds(pg * PAGE_SIZE, PAGE_SIZE), :], sem.at[0, sl]))
            cps.append(pltpu.make_async_copy(
                ckr_hbm.at[phys], rbuf.at[sl, pl.ds(pg * PAGE_SIZE, PAGE_SIZE), :], sem.at[1, sl]))
        return cps

    @pl.when(step == 0)
    def _():
        for cp in page_copies(b, c, slot):
            cp.start()

    nxt = step + 1

    @pl.when(nxt < nb * nc)
    def _():
        for cp in page_copies(nxt // nc, nxt % nc, 1 - slot):
            cp.start()

    ql = qlat_ref[0]
    qr = qrope_ref[0]

    @pl.when(c == 0)
    def _():
        cn = ckvn_ref[0]
        s0 = (jnp.sum(ql.astype(F32) * cn, axis=1, keepdims=True)
              + jnp.sum(qr * krn_ref[0], axis=1, keepdims=True))
        m_ref[...] = jnp.broadcast_to(s0, (N_HEADS, LANES))
        l_ref[...] = jnp.ones((N_HEADS, LANES), F32)
        acc_ref[...] = jnp.broadcast_to(cn, (N_HEADS, KV_RANK))

    for cp in page_copies(b, c, slot):
        cp.wait()

    kb = kbuf[slot].astype(BF16)
    rb = rbuf[slot].astype(BF16)
    s = (lax.dot_general(ql, kb, nt_dims, preferred_element_type=F32)
         + lax.dot_general(qr.astype(BF16), rb, nt_dims, preferred_element_type=F32))
    m_prev = m_ref[...]
    m_new = jnp.maximum(m_prev, jnp.max(s, axis=1, keepdims=True))
    alpha = jnp.exp(m_prev - m_new)
    p = jnp.exp(s - m_new[:, 0:1])
    l_ref[...] = alpha * l_ref[...] + jnp.sum(p, axis=1, keepdims=True)
    m_ref[...] = m_new
    pv = jnp.dot(p.astype(BF16), kb, preferred_element_type=F32)
    acc_ref[...] = acc_ref[...] * jnp.concatenate([alpha] * (KV_RANK // LANES), axis=1) + pv

    @pl.when(c == nc - 1)
    def _():
        inv = 1.0 / l_ref[...]
        o_ref[0] = acc_ref[...] * jnp.concatenate([inv] * (KV_RANK // LANES), axis=1)


def _sattn(page_table, qlat, qrope, ckv_new, kr_new, cache_ckv, cache_kr, pc):
    n, n_pages = page_table.shape
    nc = n_pages // pc
    grid_spec = pltpu.PrefetchScalarGridSpec(
        num_scalar_prefetch=1,
        grid=(n, nc),
        in_specs=[
            pl.BlockSpec((1, N_HEADS, KV_RANK), lambda b, c, pt: (b, 0, 0)),
            pl.BlockSpec((1, N_HEADS, ROPE_DIM), lambda b, c, pt: (b, 0, 0)),
            pl.BlockSpec((1, 1, KV_RANK), lambda b, c, pt: (b, 0, 0)),
            pl.BlockSpec((1, 1, ROPE_DIM), lambda b, c, pt: (b, 0, 0)),
            pl.BlockSpec(memory_space=pl.ANY),
            pl.BlockSpec(memory_space=pl.ANY),
        ],
        out_specs=pl.BlockSpec((1, N_HEADS, KV_RANK), lambda b, c, pt: (b, 0, 0)),
        scratch_shapes=[
            pltpu.VMEM((2, pc * PAGE_SIZE, KV_RANK), F32),
            pltpu.VMEM((2, pc * PAGE_SIZE, ROPE_DIM), F32),
            pltpu.SemaphoreType.DMA((2, 2)),
            pltpu.VMEM((N_HEADS, LANES), F32),
            pltpu.VMEM((N_HEADS, LANES), F32),
            pltpu.VMEM((N_HEADS, KV_RANK), F32),
        ],
    )
    return pl.pallas_call(
        functools.partial(_sattn_kernel, pc=pc, nc=nc),
        out_shape=jax.ShapeDtypeStruct((n, N_HEADS, KV_RANK), F32),
        grid_spec=grid_spec,
        compiler_params=pltpu.CompilerParams(
            dimension_semantics=("arbitrary", "arbitrary"), vmem_limit_bytes=VMEM_LIMIT),
        name="sattn",
    )(page_table.reshape(-1), qlat, qrope, ckv_new, kr_new, cache_ckv, cache_kr)


def _smerge_kernel(o_ref, wuv_ref, sga_ref, gbb_ref, out_ref):
    a = jnp.concatenate(
        [jnp.dot(o_ref[h].astype(BF16), wuv_ref[h], preferred_element_type=F32)
         for h in range(N_HEADS)], axis=1)
    out_ref[...] = (sga_ref[...].astype(F32) * a + gbb_ref[...].astype(F32)).astype(BF16)


def _smerge(o_t, sga, gbb, wts):
    n = sga.shape[0]
    return pl.pallas_call(
        _smerge_kernel,
        out_shape=jax.ShapeDtypeStruct((n, D_MODEL), BF16),
        name="smerge",
    )(o_t, wts["w_uv"], sga, gbb)


def _post_kernel(x_ref, mg_ref, wout_ref, g2_ref, wr_ref, br_ref, x1_ref, h2_ref, ti_ref, tg_ref):
    tm = x_ref.shape[0]
    x1 = x_ref[...] + jnp.dot(mg_ref[...], wout_ref[...], preferred_element_type=F32)
    x1_ref[...] = x1
    h2 = _rms(x1, g2_ref[...])
    h2_ref[...] = h2
    hh = h2.astype(BF16)
    hl = (h2 - hh.astype(F32)).astype(BF16)
    o1 = jnp.dot(hh, wr_ref[...], preferred_element_type=F32)
    o2 = jnp.dot(hl, wr_ref[:, 0:LANES], preferred_element_type=F32)
    logits = o1[:, 0:LANES] + o1[:, LANES:] + o2 + br_ref[...]
    lane = lax.broadcasted_iota(jnp.int32, (tm, LANES), 1)
    vals = logits
    tops, idxs = [], []
    for _ in range(TOP_K):
        m = jnp.max(vals, axis=1, keepdims=True)
        idx = jnp.min(jnp.where(vals == m, lane, LANES), axis=1, keepdims=True)
        tops.append(m)
        idxs.append(idx)
        vals = jnp.where(lane == idx, NEG_BIG * 2, vals)
    es = [jnp.exp(v - tops[0]) for v in tops]
    den = es[0] + es[1] + es[2] + es[3]
    ti = jnp.zeros((tm, LANES), jnp.int32)
    tg = jnp.zeros((tm, LANES), F32)
    for k in range(TOP_K):
        ti = jnp.where(lane == k, idxs[k], ti)
        tg = jnp.where(lane == k, es[k] / den, tg)
    ti_ref[...] = ti
    tg_ref[...] = tg


def _post(x, merged, wts, tm):
    T = x.shape[0]
    full = lambda shape: pl.BlockSpec(shape, lambda i: (0,) * len(shape))
    row = lambda w: pl.BlockSpec((tm, w), lambda i: (i, 0))
    return pl.pallas_call(
        _post_kernel,
        out_shape=(
            jax.ShapeDtypeStruct((T, D_MODEL), F32),
            jax.ShapeDtypeStruct((T, D_MODEL), F32),
            jax.ShapeDtypeStruct((T, LANES), jnp.int32),
            jax.ShapeDtypeStruct((T, LANES), F32),
        ),
        grid=(T // tm,),
        in_specs=[row(D_MODEL), row(D_MODEL), full((D_MODEL, D_MODEL)), full((1, D_MODEL)),
                  full((D_MODEL, 2 * LANES)), full((1, LANES))],
        out_specs=(row(D_MODEL), row(D_MODEL), row(LANES), row(LANES)),
        compiler_params=pltpu.CompilerParams(
            dimension_semantics=("arbitrary",), vmem_limit_bytes=VMEM_LIMIT),
        name="post",
    )(x, merged, wts["w_out"], wts["g2"], wts["w_r"], wts["b_r"])


def _experts_kernel(te_ref, tf_ref, tv_ref, xs_ref, wgu_ref, bgu_ref, wdn_ref, bdn_ref, out_ref,
                    wgu_bf, wdn_bf):
    t = pl.program_id(0)

    @pl.when(tf_ref[t] == 1)
    def _():
        wgu_bf[...] = wgu_ref[0].astype(BF16)
        wdn_bf[...] = wdn_ref[0].astype(BF16)

    @pl.when(tv_ref[t] == 1)
    def _():
        gu = jnp.dot(xs_ref[...], wgu_bf[...], preferred_element_type=F32) + bgu_ref[0]
        gate = jnp.minimum(gu[:, 0:D_FF], SWIGLU_LIMIT)
        up = jnp.clip(gu[:, D_FF:], -SWIGLU_LIMIT, SWIGLU_LIMIT)
        act = gate * _sigmoid(gate * SWIGLU_ALPHA)
        hmid = ((up + 1.0) * act).astype(BF16)
        out_ref[...] = jnp.dot(hmid, wdn_bf[...], preferred_element_type=F32) + bdn_ref[0]

    @pl.when(tv_ref[t] == 0)
    def _():
        out_ref[...] = jnp.zeros_like(out_ref)


def _experts(tile_e, tile_first, tile_valid, xs, w_gate_up, b_gate_up, w_down, b_down, tm):
    cap = xs.shape[0]
    n_tiles = cap // tm
    grid_spec = pltpu.PrefetchScalarGridSpec(
        num_scalar_prefetch=3,
        grid=(n_tiles,),
        in_specs=[
            pl.BlockSpec((tm, D_MODEL), lambda t, te, tf, tv: (t, 0)),
            pl.BlockSpec((1, D_MODEL, 2 * D_FF), lambda t, te, tf, tv: (te[t], 0, 0)),
            pl.BlockSpec((1, 1, 2 * D_FF), lambda t, te, tf, tv: (te[t], 0, 0)),
            pl.BlockSpec((1, D_FF, D_MODEL), lambda t, te, tf, tv: (te[t], 0, 0)),
            pl.BlockSpec((1, 1, D_MODEL), lambda t, te, tf, tv: (te[t], 0, 0)),
        ],
        out_specs=pl.BlockSpec((tm, D_MODEL), lambda t, te, tf, tv: (t, 0)),
        scratch_shapes=[pltpu.VMEM((D_MODEL, 2 * D_FF), BF16), pltpu.VMEM((D_FF, D_MODEL), BF16)],
    )
    return pl.pallas_call(
        _experts_kernel,
        out_shape=jax.ShapeDtypeStruct((cap, D_MODEL), F32),
        grid_spec=grid_spec,
        compiler_params=pltpu.CompilerParams(
            dimension_semantics=("arbitrary",), vmem_limit_bytes=VMEM_LIMIT),
        name="experts",
    )(tile_e, tile_first, tile_valid, xs, w_gate_up, b_gate_up.reshape(N_EXPERTS, 1, 2 * D_FF),
      w_down, b_down.reshape(N_EXPERTS, 1, D_MODEL))


def _combine_kernel(x1_ref, yg_ref, tg_ref, gf_ref, out_ref):
    y = x1_ref[...]
    tg = tg_ref[...]
    for k in range(TOP_K):
        y = y + yg_ref[:, k, :] * tg[:, k:k + 1]
    out_ref[...] = _rms(y, gf_ref[...])


def _combine(x1, yg, tg, gf, tm):
    T = x1.shape[0]
    return pl.pallas_call(
        _combine_kernel,
        out_shape=jax.ShapeDtypeStruct((T, D_MODEL), F32),
        grid=(T // tm,),
        in_specs=[
            pl.BlockSpec((tm, D_MODEL), lambda i: (i, 0)),
            pl.BlockSpec((tm, TOP_K, D_MODEL), lambda i: (i, 0, 0)),
            pl.BlockSpec((tm, LANES), lambda i: (i, 0)),
            pl.BlockSpec((1, D_MODEL), lambda i: (0, 0)),
        ],
        out_specs=pl.BlockSpec((tm, D_MODEL), lambda i: (i, 0)),
        compiler_params=pltpu.CompilerParams(
            dimension_semantics=("arbitrary",), vmem_limit_bytes=VMEM_LIMIT),
        name="combine",
    )(x1, yg, tg, gf)


def _rope_tables(pos):
    inv = ROPE_BASE ** (-jnp.arange(ROPE_HALF, dtype=F32) / ROPE_HALF)
    ang = pos.astype(F32)[:, None] * inv[None, :]
    return jnp.tile(jnp.cos(ang), (1, N_HEADS)), jnp.tile(jnp.sin(ang), (1, N_HEADS))


def _prep_weights(norm1_g, w_in, q_norm_g, w_uq, kv_norm_g, w_uk, w_uv, w_pool, pool_scale, w_out,
                  norm2_g, w_router, b_router):
    i0 = Q_RANK
    i1 = i0 + KV_RANK
    i2 = i1 + ROPE_DIM
    i3 = i2 + POOL_WIDTH
    w_packed = jnp.concatenate([
        w_in[:, 0:i1],
        jnp.tile(w_in[:, i1:i1 + ROPE_HALF], (1, N_HEADS)),
        jnp.tile(w_in[:, i1 + ROPE_HALF:i2], (1, N_HEADS)),
        w_in[:, i2:],
    ], axis=1).astype(BF16)
    wq = w_uq.reshape(Q_RANK, N_HEADS, QK_DIM)
    wq_packed = jnp.concatenate([
        wq[:, :, 0:NOPE_DIM].reshape(Q_RANK, Q_NOPE_W),
        wq[:, :, NOPE_DIM:NOPE_DIM + ROPE_HALF].reshape(Q_RANK, LANES),
        wq[:, :, NOPE_DIM + ROPE_HALF:].reshape(Q_RANK, LANES),
    ], axis=1).astype(BF16)
    zeros = jnp.zeros((NOPE_DIM, KV_RANK), F32)
    w_ukp = jnp.stack([
        jnp.concatenate([
            jnp.concatenate([w_uk[2 * jp], zeros], axis=1),
            jnp.concatenate([zeros, w_uk[2 * jp + 1]], axis=1)], axis=0)
        for jp in range(N_HEADS // 2)]).astype(BF16)
    wr_hi = w_router.astype(BF16)
    wr_lo = (w_router - wr_hi.astype(F32)).astype(BF16)
    pad = ((0, 0), (0, LANES - N_EXPERTS))
    w_r = jnp.concatenate([jnp.pad(wr_hi, pad), jnp.pad(wr_lo, pad)], axis=1)
    b_r = jnp.concatenate([b_router.astype(F32), jnp.full((LANES - N_EXPERTS,), NEG_BIG, F32)])[None, :]
    return {
        "g1": norm1_g[None, :], "w_in": w_packed, "qg": q_norm_g[None, :], "w_uq": wq_packed,
        "kvg": kv_norm_g[None, :], "w_ukp": w_ukp, "w_uv": w_uv.astype(BF16),
        "w_pool": w_pool.astype(BF16), "pool_scale": pool_scale[None, :], "w_out": w_out.astype(BF16),
        "g2": norm2_g[None, :], "w_r": w_r, "b_r": b_r,
    }


def _route(top_i, tm):
    T = top_i.shape[0]
    n_asg = T * TOP_K
    n_tiles = -(-n_asg // tm) + N_EXPERTS
    e = top_i.reshape(-1)
    onehot = (e[:, None] == jnp.arange(N_EXPERTS, dtype=jnp.int32)[None, :]).astype(jnp.int32)
    csum = jnp.cumsum(onehot, axis=0)
    rank = jnp.take_along_axis(csum, e[:, None], axis=1)[:, 0] - 1
    counts = csum[-1]
    padded = (counts + tm - 1) // tm * tm
    pend = jnp.cumsum(padded)
    pstart = pend - padded
    dest = pstart[e] + rank
    tok = jnp.arange(n_asg, dtype=jnp.int32) // TOP_K
    slot_tok = jnp.zeros((n_tiles * tm,), jnp.int32).at[dest].set(tok)
    tile_start = jnp.arange(n_tiles, dtype=jnp.int32) * tm
    tile_e = jnp.minimum(jnp.searchsorted(pend, tile_start, side="right"), N_EXPERTS - 1).astype(jnp.int32)
    tile_valid = (tile_start < pend[-1]).astype(jnp.int32)
    tile_first = jnp.concatenate([jnp.ones((1,), jnp.int32), (tile_e[1:] != tile_e[:-1]).astype(jnp.int32)])
    return dest.reshape(T, TOP_K), slot_tok, tile_e, tile_first, tile_valid


def kernel(x_prompt, x_sample, cache_ckv, cache_krope, state_pool, page_table, meta_tokens, norm1_g, w_in, q_norm_g, w_uq, kv_norm_g, w_uk, w_uv, w_pool, pool_scale, w_out, norm2_g, w_router, b_router, w_gate_up, b_gate_up, w_down, b_down, final_norm_g):
    B, L, _ = x_prompt.shape
    n = x_sample.shape[0]
    n_pages = page_table.shape[1]
    past_len = n_pages * PAGE_SIZE
    wts = _prep_weights(norm1_g[0], w_in[0], q_norm_g[0], w_uq[0], kv_norm_g[0], w_uk[0], w_uv[0],
                        w_pool[0], pool_scale[0], w_out[0], norm2_g[0], w_router[0], b_router[0])

    cos_m, sin_m = _rope_tables(jnp.arange(N_META))
    _, kcat_m, ckv_m, kr_m, p_m, _, _ = _front(
        meta_tokens[None], cos_m, sin_m, jnp.zeros((16, POOL_WIDTH), F32), wts, N_META)

    cos_p, sin_p = _rope_tables(N_META + jnp.arange(L))
    q, kcat, ckv_p, kr_p, ptail, sga, gbb = _front(x_prompt, cos_p, sin_p, p_m[0], wts, FRONT_TM)
    kmeta = jnp.pad(kcat_m[0], ((0, LANES - N_META), (0, 0)))
    merged_p = _attn(q, kcat, kmeta, sga, gbb, wts, ATTN_T)

    cos_s, sin_s = _rope_tables(jnp.full((n,), past_len))
    hist_t = jnp.transpose(state_pool[0], (1, 0, 2))
    xs = x_sample[:, 0, :]
    qlat_t, q1_s, q2_s, ckv_s, kr_s, p_s, sga_s, gbb_s = _sfront(xs, cos_s, sin_s, hist_t, wts)
    qlat_s = jnp.transpose(qlat_t, (1, 0, 2))
    qrope_s = jnp.concatenate([q1_s.reshape(n, N_HEADS, ROPE_HALF), q2_s.reshape(n, N_HEADS, ROPE_HALF)], axis=2)
    o_s = _sattn(page_table, qlat_s, qrope_s, ckv_s[:, None, :], kr_s[:, None, :],
                 cache_ckv[0], cache_krope[0], min(SATTN_PAGES, n_pages))
    merged_s = _smerge(jnp.transpose(o_s, (1, 0, 2)), sga_s, gbb_s, wts)

    Tp = B * L
    n_pad = -(-n // POST_TM) * POST_TM
    x_all = jnp.concatenate([x_prompt.reshape(Tp, D_MODEL), xs, jnp.zeros((n_pad - n, D_MODEL), F32)], axis=0)
    mg_all = jnp.concatenate([merged_p.reshape(Tp, D_MODEL), merged_s,
                              jnp.zeros((n_pad - n, D_MODEL), BF16)], axis=0)
    x1, h2, ti, tg = _post(x_all, mg_all, wts, POST_TM)
    T = Tp + n
    dest, slot_tok, tile_e, tile_first, tile_valid = _route(ti[:T, 0:TOP_K], MOE_TM)
    xs_g = jnp.take(h2, slot_tok, axis=0).astype(BF16)
    yp = _experts(tile_e, tile_first, tile_valid, xs_g, w_gate_up[0], b_gate_up[0], w_down[0], b_down[0], MOE_TM)
    dest_pad = jnp.concatenate([dest, jnp.zeros((n_pad - n, TOP_K), jnp.int32)], axis=0)
    yg = jnp.take(yp, dest_pad.reshape(-1), axis=0).reshape(Tp + n_pad, TOP_K, D_MODEL)
    y = _combine(x1, yg, tg, final_norm_g[None, :], POST_TM)

    y_prompt = y[:Tp].reshape(B, L, D_MODEL)
    y_sample = y[Tp:T].reshape(n, 1, D_MODEL)
    new_ckv_prompt = jnp.concatenate([jnp.broadcast_to(ckv_m, (B, N_META, KV_RANK)), ckv_p], axis=1)[None]
    new_krope_prompt = jnp.concatenate([jnp.broadcast_to(kr_m, (B, N_META, ROPE_DIM)), kr_p], axis=1)[None]
    new_pool_prompt = ptail[:, 16 - POOL_HIST:, :][None]
    new_ckv_sample = ckv_s[None, :, None, :]
    new_krope_sample = kr_s[None, :, None, :]
    new_pool_sample = jnp.concatenate([state_pool[0][:, 1:, :], p_s[:, None, :]], axis=1)[None]
    return (y_prompt, y_sample, new_ckv_prompt, new_krope_prompt, new_pool_prompt,
            new_ckv_sample, new_krope_sample, new_pool_sample)
```

```python
import functools
import math

import jax
import jax.numpy as jnp
from jax import lax
from jax.experimental import pallas as pl
from jax.experimental.pallas import tpu as pltpu

F32 = jnp.float32
BF16 = jnp.bfloat16

D_MODEL = 1024
N_META = 16
N_HEADS = 8
Q_RANK = 256
KV_RANK = 256
NOPE_DIM = 64
ROPE_DIM = 32
ROPE_HALF = ROPE_DIM // 2
QK_DIM = NOPE_DIM + ROPE_DIM
V_DIM = D_MODEL // N_HEADS
ROPE_BASE = 10000.0
ATTN_SCALE = 1.0 / math.sqrt(QK_DIM)
POOL_WIDTH = D_MODEL // 2
POOL_WINDOWS = (2, 4, 8, 16)
POOL_GROUP = POOL_WIDTH // len(POOL_WINDOWS)
POOL_OUT_GROUP = D_MODEL // len(POOL_WINDOWS)
POOL_HIST = max(POOL_WINDOWS) - 1
N_EXPERTS = 32
TOP_K = 4
D_FF = D_MODEL
SWIGLU_LIMIT = 7.0
SWIGLU_ALPHA = 1.702
EPS = 1e-6
PAGE_SIZE = 128

LANES = 128
NEG_BIG = -1e30

C_CQ = 0
C_CKV = C_CQ + Q_RANK
C_K1 = C_CKV + KV_RANK
C_K2 = C_K1 + LANES
C_P = C_K2 + LANES
C_GA = C_P + POOL_WIDTH
C_GB = C_GA + D_MODEL
C_END = C_GB + D_MODEL
Q_NOPE_W = N_HEADS * NOPE_DIM
Q_COLS = Q_NOPE_W + 2 * LANES
KCAT_W = KV_RANK + 2 * LANES

FRONT_TM = 256
ATTN_T = 256
POST_TM = 256
MOE_TM = 256
SATTN_PAGES = 64
ROPE_GROUP = 2 * LANES // ROPE_DIM
DISPATCH_TD = 128
VMEM_LIMIT = 56 * 1024 * 1024


def _rms(x, g):
    return x * lax.rsqrt(jnp.mean(x * x, axis=-1, keepdims=True) + EPS) * g


def _sigmoid(x):
    return 1.0 / (1.0 + jnp.exp(-x))


def _rope_pair_to_rows(k1r, k2r):
    lane = lax.broadcasted_iota(jnp.int32, k1r.shape, 1)
    return jnp.where(lane < ROPE_HALF, k1r, pltpu.roll(k2r, ROPE_HALF, axis=1))[:, 0:ROPE_DIM]


def _front_kernel(x_ref, g1_ref, w_ref, qg_ref, wuq_ref, kvg_ref, cos_ref, sin_ref, wpool_ref,
                  pscale_ref, pprev_ref,
                  q_ref, kcat_ref, ckv_ref, kr_ref, ptail_ref, sga_ref, gbb_ref,
                  pext_ref, *, tm):
    i = pl.program_id(1)
    x = x_ref[0]
    hb = _rms(x, g1_ref[...]).astype(BF16)
    z = jnp.dot(hb, w_ref[...], preferred_element_type=F32)
    cos = cos_ref[...]
    sin = sin_ref[...]

    qn = _rms(z[:, C_CQ:C_CQ + Q_RANK], qg_ref[...]).astype(BF16)
    q = jnp.dot(qn, wuq_ref[...], preferred_element_type=F32) * ATTN_SCALE
    q1 = q[:, Q_NOPE_W:Q_NOPE_W + LANES]
    q2 = q[:, Q_NOPE_W + LANES:]
    q_ref[0, :, 0:Q_NOPE_W] = q[:, 0:Q_NOPE_W].astype(BF16)
    q_ref[0, :, Q_NOPE_W:Q_NOPE_W + LANES] = (q1 * cos - q2 * sin).astype(BF16)
    q_ref[0, :, Q_NOPE_W + LANES:] = (q1 * sin + q2 * cos).astype(BF16)

    ckv = _rms(z[:, C_CKV:C_CKV + KV_RANK], kvg_ref[...])
    ckv_ref[0] = ckv
    k1 = z[:, C_K1:C_K1 + LANES]
    k2 = z[:, C_K2:C_K2 + LANES]
    k1r = k1 * cos - k2 * sin
    k2r = k1 * sin + k2 * cos
    kcat_ref[0, :, 0:KV_RANK] = ckv.astype(BF16)
    kcat_ref[0, :, KV_RANK:KV_RANK + LANES] = k1r.astype(BF16)
    kcat_ref[0, :, KV_RANK + LANES:] = k2r.astype(BF16)
    kr_ref[0] = _rope_pair_to_rows(k1r, k2r)

    p = z[:, C_P:C_P + POOL_WIDTH]

    @pl.when(i == 0)
    def _():
        pext_ref[0:16, :] = pprev_ref[...]

    @pl.when(i > 0)
    def _():
        pext_ref[0:16, :] = pext_ref[tm:tm + 16, :]

    pext_ref[16:16 + tm, :] = p
    ptail_ref[0] = p[tm - 16:tm, :]
    bs = []
    for g, w in enumerate(POOL_WINDOWS):
        lo = g * POOL_GROUP
        acc = pext_ref[16:16 + tm, lo:lo + POOL_GROUP]
        for k in range(1, w):
            acc = acc + pext_ref[16 - k:16 - k + tm, lo:lo + POOL_GROUP]
        pooled = acc * (1.0 / w) - p[:, lo:lo + POOL_GROUP]
        bs.append(jnp.dot(pooled.astype(BF16), wpool_ref[g], preferred_element_type=F32))
    b = jnp.concatenate(bs, axis=1) * pscale_ref[...]
    sga_ref[0] = _sigmoid(z[:, C_GA:C_GA + D_MODEL]).astype(BF16)
    gbb_ref[0] = (_sigmoid(z[:, C_GB:C_GB + D_MODEL]) * b).astype(BF16)


def _front(x, cos_t, sin_t, pprev, wts, tm):
    B, L, _ = x.shape
    nt = L // tm
    full = lambda shape: pl.BlockSpec(shape, lambda b, i: (0,) * len(shape))
    row = lambda w: pl.BlockSpec((1, tm, w), lambda b, i: (b, i, 0))
    out_shape = (
        jax.ShapeDtypeStruct((B, L, Q_COLS), BF16),
        jax.ShapeDtypeStruct((B, L, KCAT_W), BF16),
        jax.ShapeDtypeStruct((B, L, KV_RANK), F32),
        jax.ShapeDtypeStruct((B, L, ROPE_DIM), F32),
        jax.ShapeDtypeStruct((B, 16, POOL_WIDTH), F32),
        jax.ShapeDtypeStruct((B, L, D_MODEL), BF16),
        jax.ShapeDtypeStruct((B, L, D_MODEL), BF16),
    )
    return pl.pallas_call(
        functools.partial(_front_kernel, tm=tm),
        out_shape=out_shape,
        grid=(B, nt),
        in_specs=[
            row(D_MODEL),
            full((1, D_MODEL)),
            full((D_MODEL, C_END)),
            full((1, Q_RANK)),
            full((Q_RANK, Q_COLS)),
            full((1, KV_RANK)),
            pl.BlockSpec((tm, LANES), lambda b, i: (i, 0)),
            pl.BlockSpec((tm, LANES), lambda b, i: (i, 0)),
            full((len(POOL_WINDOWS), POOL_GROUP, POOL_OUT_GROUP)),
            full((1, D_MODEL)),
            full((16, POOL_WIDTH)),
        ],
        out_specs=(
            row(Q_COLS), row(KCAT_W), row(KV_RANK), row(ROPE_DIM),
            pl.BlockSpec((1, 16, POOL_WIDTH), lambda b, i: (b, 0, 0)),
            row(D_MODEL), row(D_MODEL),
        ),
        scratch_shapes=[pltpu.VMEM((tm + 16, POOL_WIDTH), F32)],
        compiler_params=pltpu.CompilerParams(
            dimension_semantics=("arbitrary", "arbitrary"), vmem_limit_bytes=VMEM_LIMIT),
        name="front",
    )(x, wts["g1"], wts["w_in"], wts["qg"], wts["w_uq"], wts["kvg"], cos_t, sin_t, wts["w_pool"],
      wts["pool_scale"], pprev)


def _build_qcat(q, wukp_ref, qcat_ref, t):
    lane = lax.broadcasted_iota(jnp.int32, (t, LANES), 1)
    q1 = q[:, Q_NOPE_W:Q_NOPE_W + LANES].astype(F32)
    q2 = q[:, Q_NOPE_W + LANES:].astype(F32)
    zero = jnp.zeros_like(q1)
    for jp in range(N_HEADS // 2):
        ql = jnp.dot(q[:, jp * LANES:(jp + 1) * LANES], wukp_ref[jp], preferred_element_type=F32)
        for s in range(2):
            h = 2 * jp + s
            sel = (lane >= h * ROPE_HALF) & (lane < (h + 1) * ROPE_HALF)
            qcat_ref[h * t:(h + 1) * t, 0:KV_RANK] = ql[:, s * KV_RANK:(s + 1) * KV_RANK].astype(BF16)
            qcat_ref[h * t:(h + 1) * t, KV_RANK:KV_RANK + LANES] = jnp.where(sel, q1, zero).astype(BF16)
            qcat_ref[h * t:(h + 1) * t, KV_RANK + LANES:] = jnp.where(sel, q2, zero).astype(BF16)


def _attn_kernel(qi_ref, kj_ref, q_ref, kcat_ref, kmeta_ref, wukp_ref, wuv_ref, sga_ref, gbb_ref,
                 out_ref, qcat_ref, m_ref, l_ref, acc_ref, *, t):
    s_id = pl.program_id(1)
    i = qi_ref[s_id]
    j = kj_ref[s_id]
    rows = N_HEADS * t
    nt_dims = (((1,), (1,)), ((), ()))

    @pl.when(j == 0)
    def _():
        _build_qcat(q_ref[0], wukp_ref, qcat_ref, t)
        km = kmeta_ref[...]
        s = lax.dot_general(qcat_ref[...], km, nt_dims, preferred_element_type=F32)
        col = lax.broadcasted_iota(jnp.int32, (rows, LANES), 1)
        s = jnp.where(col < N_META, s, NEG_BIG)
        m = jnp.max(s, axis=1, keepdims=True)
        p = jnp.exp(s - m)
        m_ref[...] = jnp.broadcast_to(m, (rows, LANES))
        l_ref[...] = jnp.broadcast_to(jnp.sum(p, axis=1, keepdims=True), (rows, LANES))
        acc_ref[...] = jnp.dot(p.astype(BF16), km[:, 0:KV_RANK], preferred_element_type=F32)

    k = kcat_ref[0]
    s = lax.dot_general(qcat_ref[...], k, nt_dims, preferred_element_type=F32)
    r_tok = lax.broadcasted_iota(jnp.int32, (rows, t), 0) & (t - 1)
    col = lax.broadcasted_iota(jnp.int32, (rows, t), 1)
    s = jnp.where(col - r_tok <= (i - j) * t, s, NEG_BIG)
    m_prev = m_ref[...]
    m_new = jnp.maximum(m_prev, jnp.max(s, axis=1, keepdims=True))
    alpha = jnp.exp(m_prev - m_new)
    p = jnp.exp(s - jnp.concatenate([m_new] * (t // LANES), axis=1))
    l_ref[...] = alpha * l_ref[...] + jnp.sum(p, axis=1, keepdims=True)
    m_ref[...] = m_new
    pv = jnp.dot(p.astype(BF16), k[:, 0:KV_RANK], preferred_element_type=F32)
    acc_ref[...] = acc_ref[...] * jnp.concatenate([alpha] * (KV_RANK // LANES), axis=1) + pv

    @pl.when(j == i)
    def _():
        inv = 1.0 / l_ref[...]
        o = acc_ref[...] * jnp.concatenate([inv] * (KV_RANK // LANES), axis=1)
        a = jnp.concatenate(
            [jnp.dot(o[h * t:(h + 1) * t].astype(BF16), wuv_ref[h], preferred_element_type=F32)
             for h in range(N_HEADS)], axis=1)
        merged = sga_ref[0].astype(F32) * a + gbb_ref[0].astype(F32)
        out_ref[0] = merged.astype(BF16)


def _attn(q, kcat, kmeta, sga, gbb, wts, t):
    B, L, _ = q.shape
    nq = L // t
    pairs = [(i, j) for i in range(nq) for j in range(i + 1)]
    qi = jnp.asarray([p[0] for p in pairs], jnp.int32)
    kj = jnp.asarray([p[1] for p in pairs], jnp.int32)
    rows = N_HEADS * t
    grid_spec = pltpu.PrefetchScalarGridSpec(
        num_scalar_prefetch=2,
        grid=(B, len(pairs)),
        in_specs=[
            pl.BlockSpec((1, t, Q_COLS), lambda b, s, qi, kj: (b, qi[s], 0)),
            pl.BlockSpec((1, t, KCAT_W), lambda b, s, qi, kj: (b, kj[s], 0)),
            pl.BlockSpec((LANES, KCAT_W), lambda b, s, qi, kj: (0, 0)),
            pl.BlockSpec((N_HEADS // 2, LANES, 2 * KV_RANK), lambda b, s, qi, kj: (0, 0, 0)),
            pl.BlockSpec((N_HEADS, KV_RANK, V_DIM), lambda b, s, qi, kj: (0, 0, 0)),
            pl.BlockSpec((1, t, D_MODEL), lambda b, s, qi, kj: (b, qi[s], 0)),
            pl.BlockSpec((1, t, D_MODEL), lambda b, s, qi, kj: (b, qi[s], 0)),
        ],
        out_specs=pl.BlockSpec((1, t, D_MODEL), lambda b, s, qi, kj: (b, qi[s], 0)),
        scratch_shapes=[
            pltpu.VMEM((rows, KCAT_W), BF16),
            pltpu.VMEM((rows, LANES), F32),
            pltpu.VMEM((rows, LANES), F32),
            pltpu.VMEM((rows, KV_RANK), F32),
        ],
    )
    return pl.pallas_call(
        functools.partial(_attn_kernel, t=t),
        out_shape=jax.ShapeDtypeStruct((B, L, D_MODEL), BF16),
        grid_spec=grid_spec,
        compiler_params=pltpu.CompilerParams(
            dimension_semantics=("arbitrary", "arbitrary"), vmem_limit_bytes=VMEM_LIMIT),
        name="attn",
    )(qi, kj, q, kcat, kmeta, wts["w_ukp"], wts["w_uv"], sga, gbb)


def _sfront_kernel(x_ref, g1_ref, w_ref, qg_ref, wuq_ref, kvg_ref, cos_ref, sin_ref, wpool_ref,
                   pscale_ref, hist_ref, wukp_ref,
                   qlat_ref, q1_ref, q2_ref, ckv_ref, kr_ref, p_ref, sga_ref, gbb_ref):
    n = x_ref.shape[0]
    x = x_ref[...]
    hb = _rms(x, g1_ref[...]).astype(BF16)
    z = jnp.dot(hb, w_ref[...], preferred_element_type=F32)
    cos = cos_ref[...]
    sin = sin_ref[...]

    qn = _rms(z[:, C_CQ:C_CQ + Q_RANK], qg_ref[...]).astype(BF16)
    q = jnp.dot(qn, wuq_ref[...], preferred_element_type=F32) * ATTN_SCALE
    q1 = q[:, Q_NOPE_W:Q_NOPE_W + LANES]
    q2 = q[:, Q_NOPE_W + LANES:]
    q1_ref[...] = q1 * cos - q2 * sin
    q2_ref[...] = q1 * sin + q2 * cos
    qb = q[:, 0:Q_NOPE_W].astype(BF16)
    for jp in range(N_HEADS // 2):
        ql = jnp.dot(qb[:, jp * LANES:(jp + 1) * LANES], wukp_ref[jp], preferred_element_type=F32)
        qlat_ref[2 * jp] = ql[:, 0:KV_RANK].astype(BF16)
        qlat_ref[2 * jp + 1] = ql[:, KV_RANK:].astype(BF16)

    ckv = _rms(z[:, C_CKV:C_CKV + KV_RANK], kvg_ref[...])
    ckv_ref[...] = ckv
    k1 = z[:, C_K1:C_K1 + LANES]
    k2 = z[:, C_K2:C_K2 + LANES]
    k1r = k1 * cos - k2 * sin
    k2r = k1 * sin + k2 * cos
    kr_ref[...] = _rope_pair_to_rows(k1r, k2r)

    p = z[:, C_P:C_P + POOL_WIDTH]
    p_ref[...] = p
    bs = []
    for g, w in enumerate(POOL_WINDOWS):
        lo = g * POOL_GROUP
        acc = p[:, lo:lo + POOL_GROUP]
        for k in range(1, w):
            acc = acc + hist_ref[POOL_HIST - k, :, lo:lo + POOL_GROUP]
        pooled = acc * (1.0 / w) - p[:, lo:lo + POOL_GROUP]
        bs.append(jnp.dot(pooled.astype(BF16), wpool_ref[g], preferred_element_type=F32))
    b = jnp.concatenate(bs, axis=1) * pscale_ref[...]
    sga_ref[...] = _sigmoid(z[:, C_GA:C_GA + D_MODEL]).astype(BF16)
    gbb_ref[...] = (_sigmoid(z[:, C_GB:C_GB + D_MODEL]) * b).astype(BF16)


def _sfront(xs, cos_s, sin_s, hist_t, wts):
    n = xs.shape[0]
    out_shape = (
        jax.ShapeDtypeStruct((N_HEADS, n, KV_RANK), BF16),
        jax.ShapeDtypeStruct((n, LANES), F32),
        jax.ShapeDtypeStruct((n, LANES), F32),
        jax.ShapeDtypeStruct((n, KV_RANK), F32),
        jax.ShapeDtypeStruct((n, ROPE_DIM), F32),
        jax.ShapeDtypeStruct((n, POOL_WIDTH), F32),
        jax.ShapeDtypeStruct((n, D_MODEL), BF16),
        jax.ShapeDtypeStruct((n, D_MODEL), BF16),
    )
    return pl.pallas_call(
        _sfront_kernel,
        out_shape=out_shape,
        compiler_params=pltpu.CompilerParams(vmem_limit_bytes=VMEM_LIMIT),
        name="sfront",
    )(xs, wts["g1"], wts["w_in"], wts["qg"], wts["w_uq"], wts["kvg"], cos_s, sin_s, wts["w_pool"],
      wts["pool_scale"], hist_t, wts["w_ukp"])


def _sattn_kernel(pt_ref, qlat_ref, qexp_ref, qrope_ref, ckvn_ref, krn_ref, cckv_hbm, ckrt_hbm,
                  o_ref, kbuf, rbuf, sem, m_ref, l_ref, acc_ref, *, pc, nc):
    b = pl.program_id(0)
    c = pl.program_id(1)
    nb = pl.num_programs(0)
    step = b * nc + c
    slot = step % 2
    nt_dims = (((1,), (1,)), ((), ()))

    def start_pages(bb, cc, sl):
        base = bb * (nc * pc) + cc * pc

        def body(pg, carry):
            phys = pt_ref[base + pg]
            pltpu.make_async_copy(cckv_hbm.at[phys], kbuf.at[sl, pg], sem.at[0, sl]).start()
            pltpu.make_async_copy(ckrt_hbm.at[phys], rbuf.at[sl, pg], sem.at[1, sl]).start()
            return carry

        lax.fori_loop(0, pc, body, 0)

    @pl.when(step == 0)
    def _():
        start_pages(b, c, slot)

    nxt = step + 1

    @pl.when(nxt < nb * nc)
    def _():
        start_pages(nxt // nc, nxt % nc, 1 - slot)

    ql = qlat_ref[0]

    @pl.when(c == 0)
    def _():
        cn = ckvn_ref[0]
        s0 = (jnp.sum(ql.astype(F32) * cn, axis=1, keepdims=True)
              + jnp.sum(qrope_ref[0] * krn_ref[0], axis=1, keepdims=True))
        m_ref[...] = jnp.broadcast_to(s0, (N_HEADS, LANES))
        l_ref[...] = jnp.ones((N_HEADS, LANES), F32)
        acc_ref[...] = jnp.broadcast_to(cn, (N_HEADS, KV_RANK))

    pltpu.make_async_copy(cckv_hbm.at[pl.ds(0, pc)], kbuf.at[slot], sem.at[0, slot]).wait()
    pltpu.make_async_copy(ckrt_hbm.at[pl.ds(0, pc)], rbuf.at[slot], sem.at[1, slot]).wait()

    kb = kbuf[slot].reshape(pc * PAGE_SIZE, KV_RANK).astype(BF16)
    s_lat = lax.dot_general(ql, kb, nt_dims, preferred_element_type=F32)
    qexp = qexp_ref[0]
    pieces = []
    for g in range(pc // ROPE_GROUP):
        r8 = rbuf[slot, g * ROPE_GROUP:(g + 1) * ROPE_GROUP].reshape(ROPE_GROUP * ROPE_DIM, PAGE_SIZE)
        og = jnp.dot(qexp, r8.astype(BF16), preferred_element_type=F32)
        pieces += [og[pp * N_HEADS:(pp + 1) * N_HEADS] for pp in range(ROPE_GROUP)]
    s = s_lat + jnp.concatenate(pieces, axis=1)
    m_prev = m_ref[...]
    m_new = jnp.maximum(m_prev, jnp.max(s, axis=1, keepdims=True))
    alpha = jnp.exp(m_prev - m_new)
    p = jnp.exp(s - m_new[:, 0:1])
    l_ref[...] = alpha * l_ref[...] + jnp.sum(p, axis=1, keepdims=True)
    m_ref[...] = m_new
    pv = jnp.dot(p.astype(BF16), kb, preferred_element_type=F32)
    acc_ref[...] = acc_ref[...] * jnp.concatenate([alpha] * (KV_RANK // LANES), axis=1) + pv

    @pl.when(c == nc - 1)
    def _():
        inv = 1.0 / l_ref[...]
        o_ref[0] = acc_ref[...] * jnp.concatenate([inv] * (KV_RANK // LANES), axis=1)


def _sattn(page_table, qlat, qexp, qrope, ckv_new, kr_new, cache_ckv, cache_krt, pc):
    n, n_pages = page_table.shape
    nc = n_pages // pc
    grid_spec = pltpu.PrefetchScalarGridSpec(
        num_scalar_prefetch=1,
        grid=(n, nc),
        in_specs=[
            pl.BlockSpec((1, N_HEADS, KV_RANK), lambda b, c, pt: (b, 0, 0)),
            pl.BlockSpec((1, ROPE_GROUP * N_HEADS, ROPE_GROUP * ROPE_DIM), lambda b, c, pt: (b, 0, 0)),
            pl.BlockSpec((1, N_HEADS, ROPE_DIM), lambda b, c, pt: (b, 0, 0)),
            pl.BlockSpec((1, 1, KV_RANK), lambda b, c, pt: (b, 0, 0)),
            pl.BlockSpec((1, 1, ROPE_DIM), lambda b, c, pt: (b, 0, 0)),
            pl.BlockSpec(memory_space=pl.ANY),
            pl.BlockSpec(memory_space=pl.ANY),
        ],
        out_specs=pl.BlockSpec((1, N_HEADS, KV_RANK), lambda b, c, pt: (b, 0, 0)),
        scratch_shapes=[
            pltpu.VMEM((2, pc, PAGE_SIZE, KV_RANK), F32),
            pltpu.VMEM((2, pc, ROPE_DIM, PAGE_SIZE), F32),
            pltpu.SemaphoreType.DMA((2, 2)),
            pltpu.VMEM((N_HEADS, LANES), F32),
            pltpu.VMEM((N_HEADS, LANES), F32),
            pltpu.VMEM((N_HEADS, KV_RANK), F32),
        ],
    )
    return pl.pallas_call(
        functools.partial(_sattn_kernel, pc=pc, nc=nc),
        out_shape=jax.ShapeDtypeStruct((n, N_HEADS, KV_RANK), F32),
        grid_spec=grid_spec,
        compiler_params=pltpu.CompilerParams(
            dimension_semantics=("arbitrary", "arbitrary"), vmem_limit_bytes=VMEM_LIMIT),
        name="sattn",
    )(page_table.reshape(-1), qlat, qexp, qrope, ckv_new, kr_new, cache_ckv, cache_krt)


def _smerge_kernel(o_ref, wuv_ref, sga_ref, gbb_ref, out_ref):
    a = jnp.concatenate(
        [jnp.dot(o_ref[h].astype(BF16), wuv_ref[h], preferred_element_type=F32)
         for h in range(N_HEADS)], axis=1)
    out_ref[...] = (sga_ref[...].astype(F32) * a + gbb_ref[...].astype(F32)).astype(BF16)


def _smerge(o_t, sga, gbb, wts):
    n = sga.shape[0]
    return pl.pallas_call(
        _smerge_kernel,
        out_shape=jax.ShapeDtypeStruct((n, D_MODEL), BF16),
        name="smerge",
    )(o_t, wts["w_uv"], sga, gbb)


def _post_kernel(x_ref, mg_ref, wout_ref, g2_ref, wr_ref, br_ref, x1_ref, h2_ref, ti_ref, tg_ref):
    tm = x_ref.shape[0]
    x1 = x_ref[...] + jnp.dot(mg_ref[...], wout_ref[...], preferred_element_type=F32)
    x1_ref[...] = x1
    h2 = _rms(x1, g2_ref[...])
    h2_ref[...] = h2
    hh = h2.astype(BF16)
    hl = (h2 - hh.astype(F32)).astype(BF16)
    o1 = jnp.dot(hh, wr_ref[...], preferred_element_type=F32)
    o2 = jnp.dot(hl, wr_ref[:, 0:LANES], preferred_element_type=F32)
    logits = o1[:, 0:LANES] + o1[:, LANES:] + o2 + br_ref[...]
    lane = lax.broadcasted_iota(jnp.int32, (tm, LANES), 1)
    vals = logits
    tops, idxs = [], []
    for _ in range(TOP_K):
        m = jnp.max(vals, axis=1, keepdims=True)
        idx = jnp.min(jnp.where(vals == m, lane, LANES), axis=1, keepdims=True)
        tops.append(m)
        idxs.append(idx)
        vals = jnp.where(lane == idx, NEG_BIG * 2, vals)
    es = [jnp.exp(v - tops[0]) for v in tops]
    den = es[0] + es[1] + es[2] + es[3]
    ti = jnp.zeros((tm, LANES), jnp.int32)
    tg = jnp.zeros((tm, LANES), F32)
    for k in range(TOP_K):
        ti = jnp.where(lane == k, idxs[k], ti)
        tg = jnp.where(lane == k, es[k] / den, tg)
    ti_ref[...] = ti
    tg_ref[...] = tg


def _post(x, merged, wts, tm):
    T = x.shape[0]
    full = lambda shape: pl.BlockSpec(shape, lambda i: (0,) * len(shape))
    row = lambda w: pl.BlockSpec((tm, w), lambda i: (i, 0))
    return pl.pallas_call(
        _post_kernel,
        out_shape=(
            jax.ShapeDtypeStruct((T, D_MODEL), F32),
            jax.ShapeDtypeStruct((T, D_MODEL), F32),
            jax.ShapeDtypeStruct((T, LANES), jnp.int32),
            jax.ShapeDtypeStruct((T, LANES), F32),
        ),
        grid=(T // tm,),
        in_specs=[row(D_MODEL), row(D_MODEL), full((D_MODEL, D_MODEL)), full((1, D_MODEL)),
                  full((D_MODEL, 2 * LANES)), full((1, LANES))],
        out_specs=(row(D_MODEL), row(D_MODEL), row(LANES), row(LANES)),
        compiler_params=pltpu.CompilerParams(
            dimension_semantics=("arbitrary",), vmem_limit_bytes=VMEM_LIMIT),
        name="post",
    )(x, merged, wts["w_out"], wts["g2"], wts["w_r"], wts["b_r"])


def _dispatch_kernel(plan_ref, dest_ref, h2_hbm, xs_hbm, sem, psem, *, td, tile, n_tiles):
    i = pl.program_id(0)
    n = pl.num_programs(0)
    slot = i % 2

    def body(r8, c):
        for rr in range(8):
            r = r8 * 8 + rr
            for k in range(TOP_K):
                d = dest_ref[0, 0, r * TOP_K + k]
                pltpu.make_async_copy(h2_hbm.at[pl.ds(i * td + r, 1)], xs_hbm.at[pl.ds(d, 1)],
                                      sem.at[slot]).start()
        return c

    lax.fori_loop(0, td // 8, body, 0)

    def wait_tile(s):
        pltpu.make_async_copy(h2_hbm.at[pl.ds(0, td * TOP_K)], xs_hbm.at[pl.ds(0, td * TOP_K)], sem.at[s]).wait()

    @pl.when(i > 0)
    def _():
        wait_tile(1 - slot)

    @pl.when(i == n - 1)
    def _():
        wait_tile(slot)

        def per_expert(e, c):
            lo = plan_ref[e]
            hi = plan_ref[N_EXPERTS + e]

            def fill(s, cc):
                pltpu.make_async_copy(h2_hbm.at[pl.ds(0, 1)], xs_hbm.at[pl.ds(s, 1)], psem.at[0]).start()
                return cc

            def drain(s, cc):
                pltpu.make_async_copy(h2_hbm.at[pl.ds(0, 1)], xs_hbm.at[pl.ds(s, 1)], psem.at[0]).wait()
                return cc

            lax.fori_loop(lo, hi, fill, 0)
            lax.fori_loop(lo, hi, drain, 0)
            return c

        lax.fori_loop(0, N_EXPERTS, per_expert, 0)

        def per_tile(t, c):
            cp = pltpu.make_async_copy(h2_hbm.at[pl.ds(0, tile)], xs_hbm.at[pl.ds(t * tile, tile)], psem.at[0])
            cp.start()
            cp.wait()
            return c

        lax.fori_loop(plan_ref[2 * N_EXPERTS], n_tiles, per_tile, 0)


def _dispatch(plan, dest, h2, cap, td, tile):
    T = dest.shape[0]
    nt = T // td
    grid_spec = pltpu.PrefetchScalarGridSpec(
        num_scalar_prefetch=1,
        grid=(nt,),
        in_specs=[
            pl.BlockSpec((1, 1, td * TOP_K), lambda i, plan: (i, 0, 0), memory_space=pltpu.SMEM),
            pl.BlockSpec(memory_space=pl.ANY),
        ],
        out_specs=pl.BlockSpec(memory_space=pl.ANY),
        scratch_shapes=[pltpu.SemaphoreType.DMA((2,)), pltpu.SemaphoreType.DMA((1,))],
    )
    return pl.pallas_call(
        functools.partial(_dispatch_kernel, td=td, tile=tile, n_tiles=cap // tile),
        out_shape=jax.ShapeDtypeStruct((cap, D_MODEL), h2.dtype),
        grid_spec=grid_spec,
        compiler_params=pltpu.CompilerParams(dimension_semantics=("arbitrary",)),
        name="dispatch",
    )(plan, dest.reshape(nt, 1, td * TOP_K), h2)


def _experts_kernel(te_ref, tf_ref, tv_ref, ts_ref, xs_ref, wgu_ref, bgu_ref, wdn_ref, bdn_ref, out_ref,
                    wgu_bf, wdn_bf):
    t = pl.program_id(0)

    @pl.when(tf_ref[t] == 1)
    def _():
        wgu_bf[...] = wgu_ref[0].astype(BF16)
        wdn_bf[...] = wdn_ref[0].astype(BF16)

    @pl.when(tv_ref[t] == 1)
    def _():
        gu = jnp.dot(xs_ref[...].astype(BF16), wgu_bf[...], preferred_element_type=F32) + bgu_ref[0]
        gate = jnp.minimum(gu[:, 0:D_FF], SWIGLU_LIMIT)
        up = jnp.clip(gu[:, D_FF:], -SWIGLU_LIMIT, SWIGLU_LIMIT)
        act = gate * _sigmoid(gate * SWIGLU_ALPHA)
        hmid = ((up + 1.0) * act).astype(BF16)
        out_ref[...] = jnp.dot(hmid, wdn_bf[...], preferred_element_type=F32) + bdn_ref[0]

    @pl.when(tv_ref[t] == 0)
    def _():
        out_ref[...] = jnp.zeros_like(out_ref)


def _experts(tile_e, tile_first, tile_valid, tile_src, xs, w_gate_up, b_gate_up, w_down, b_down, tm):
    cap = xs.shape[0]
    n_tiles = cap // tm
    grid_spec = pltpu.PrefetchScalarGridSpec(
        num_scalar_prefetch=4,
        grid=(n_tiles,),
        in_specs=[
            pl.BlockSpec((tm, D_MODEL), lambda t, te, tf, tv, ts: (ts[t], 0)),
            pl.BlockSpec((1, D_MODEL, 2 * D_FF), lambda t, te, tf, tv, ts: (te[t], 0, 0)),
            pl.BlockSpec((1, 1, 2 * D_FF), lambda t, te, tf, tv, ts: (te[t], 0, 0)),
            pl.BlockSpec((1, D_FF, D_MODEL), lambda t, te, tf, tv, ts: (te[t], 0, 0)),
            pl.BlockSpec((1, 1, D_MODEL), lambda t, te, tf, tv, ts: (te[t], 0, 0)),
        ],
        out_specs=pl.BlockSpec((tm, D_MODEL), lambda t, te, tf, tv, ts: (t, 0)),
        scratch_shapes=[pltpu.VMEM((D_MODEL, 2 * D_FF), BF16), pltpu.VMEM((D_FF, D_MODEL), BF16)],
    )
    return pl.pallas_call(
        _experts_kernel,
        out_shape=jax.ShapeDtypeStruct((cap, D_MODEL), F32),
        grid_spec=grid_spec,
        compiler_params=pltpu.CompilerParams(
            dimension_semantics=("arbitrary",), vmem_limit_bytes=VMEM_LIMIT),
        name="experts",
    )(tile_e, tile_first, tile_valid, tile_src, xs, w_gate_up, b_gate_up.reshape(N_EXPERTS, 1, 2 * D_FF),
      w_down, b_down.reshape(N_EXPERTS, 1, D_MODEL))


def _combine_kernel(dcur_ref, dnxt_ref, x1_ref, tg_ref, gf_ref, yp_hbm, out_ref, buf, sem, *, tm):
    i = pl.program_id(0)
    n = pl.num_programs(0)
    slot = i % 2

    def start_rows(dref, sl):
        def body(r8, c):
            for rr in range(8):
                r = r8 * 8 + rr
                for k in range(TOP_K):
                    d = dref[0, 0, r * TOP_K + k]
                    pltpu.make_async_copy(yp_hbm.at[pl.ds(d, 1)], buf.at[sl, pl.ds(k * tm + r, 1)],
                                          sem.at[sl]).start()
            return c

        lax.fori_loop(0, tm // 8, body, 0)

    @pl.when(i == 0)
    def _():
        start_rows(dcur_ref, slot)

    @pl.when(i + 1 < n)
    def _():
        start_rows(dnxt_ref, 1 - slot)

    pltpu.make_async_copy(yp_hbm.at[pl.ds(0, TOP_K * tm)], buf.at[slot], sem.at[slot]).wait()
    y = x1_ref[...]
    tg = tg_ref[...]
    for k in range(TOP_K):
        y = y + buf[slot, k * tm:(k + 1) * tm, :] * tg[:, k:k + 1]
    out_ref[...] = _rms(y, gf_ref[...])


def _combine(dest, x1, tg, gf, yp, tm):
    T = x1.shape[0]
    nt = T // tm
    dest3 = dest.reshape(nt, 1, tm * TOP_K)
    return pl.pallas_call(
        functools.partial(_combine_kernel, tm=tm),
        out_shape=jax.ShapeDtypeStruct((T, D_MODEL), F32),
        grid=(nt,),
        in_specs=[
            pl.BlockSpec((1, 1, tm * TOP_K), lambda i: (i, 0, 0), memory_space=pltpu.SMEM),
            pl.BlockSpec((1, 1, tm * TOP_K), lambda i: (jnp.minimum(i + 1, nt - 1), 0, 0),
                         memory_space=pltpu.SMEM),
            pl.BlockSpec((tm, D_MODEL), lambda i: (i, 0)),
            pl.BlockSpec((tm, LANES), lambda i: (i, 0)),
            pl.BlockSpec((1, D_MODEL), lambda i: (0, 0)),
            pl.BlockSpec(memory_space=pl.ANY),
        ],
        out_specs=pl.BlockSpec((tm, D_MODEL), lambda i: (i, 0)),
        scratch_shapes=[pltpu.VMEM((2, TOP_K * tm, D_MODEL), F32), pltpu.SemaphoreType.DMA((2,))],
        compiler_params=pltpu.CompilerParams(
            dimension_semantics=("arbitrary",), vmem_limit_bytes=VMEM_LIMIT),
        name="combine",
    )(dest3, dest3, x1, tg, gf, yp)


def _rope_tables(pos):
    inv = ROPE_BASE ** (-jnp.arange(ROPE_HALF, dtype=F32) / ROPE_HALF)
    ang = pos.astype(F32)[:, None] * inv[None, :]
    return jnp.tile(jnp.cos(ang), (1, N_HEADS)), jnp.tile(jnp.sin(ang), (1, N_HEADS))


def _prep_weights(norm1_g, w_in, q_norm_g, w_uq, kv_norm_g, w_uk, w_uv, w_pool, pool_scale, w_out,
                  norm2_g, w_router, b_router):
    i0 = Q_RANK
    i1 = i0 + KV_RANK
    i2 = i1 + ROPE_DIM
    i3 = i2 + POOL_WIDTH
    w_packed = jnp.concatenate([
        w_in[:, 0:i1],
        jnp.tile(w_in[:, i1:i1 + ROPE_HALF], (1, N_HEADS)),
        jnp.tile(w_in[:, i1 + ROPE_HALF:i2], (1, N_HEADS)),
        w_in[:, i2:],
    ], axis=1).astype(BF16)
    wq = w_uq.reshape(Q_RANK, N_HEADS, QK_DIM)
    wq_packed = jnp.concatenate([
        wq[:, :, 0:NOPE_DIM].reshape(Q_RANK, Q_NOPE_W),
        wq[:, :, NOPE_DIM:NOPE_DIM + ROPE_HALF].reshape(Q_RANK, LANES),
        wq[:, :, NOPE_DIM + ROPE_HALF:].reshape(Q_RANK, LANES),
    ], axis=1).astype(BF16)
    zeros = jnp.zeros((NOPE_DIM, KV_RANK), F32)
    w_ukp = jnp.stack([
        jnp.concatenate([
            jnp.concatenate([w_uk[2 * jp], zeros], axis=1),
            jnp.concatenate([zeros, w_uk[2 * jp + 1]], axis=1)], axis=0)
        for jp in range(N_HEADS // 2)]).astype(BF16)
    wr_hi = w_router.astype(BF16)
    wr_lo = (w_router - wr_hi.astype(F32)).astype(BF16)
    pad = ((0, 0), (0, LANES - N_EXPERTS))
    w_r = jnp.concatenate([jnp.pad(wr_hi, pad), jnp.pad(wr_lo, pad)], axis=1)
    b_r = jnp.concatenate([b_router.astype(F32), jnp.full((LANES - N_EXPERTS,), NEG_BIG, F32)])[None, :]
    return {
        "g1": norm1_g[None, :], "w_in": w_packed, "qg": q_norm_g[None, :], "w_uq": wq_packed,
        "kvg": kv_norm_g[None, :], "w_ukp": w_ukp, "w_uv": w_uv.astype(BF16),
        "w_pool": w_pool.astype(BF16), "pool_scale": pool_scale[None, :], "w_out": w_out.astype(BF16),
        "g2": norm2_g[None, :], "w_r": w_r, "b_r": b_r,
    }


def _route(top_i, tm):
    T = top_i.shape[0]
    n_asg = T * TOP_K
    n_tiles = -(-n_asg // tm) + N_EXPERTS
    e = top_i.reshape(-1)
    onehot = (e[:, None] == jnp.arange(N_EXPERTS, dtype=jnp.int32)[None, :]).astype(jnp.int32)
    csum = jnp.cumsum(onehot, axis=0)
    counts = csum[-1]
    padded = (counts + tm - 1) // tm * tm
    pend = jnp.cumsum(padded)
    pstart = pend - padded
    dest = jnp.sum(onehot * (csum - 1 + pstart[None, :]), axis=1)
    tile_start = jnp.arange(n_tiles, dtype=jnp.int32) * tm
    tile_e = jnp.minimum(jnp.sum((pend[None, :] <= tile_start[:, None]).astype(jnp.int32), axis=1), N_EXPERTS - 1)
    n_used = pend[-1] // tm
    tile_valid = (tile_start < pend[-1]).astype(jnp.int32)
    tile_first = jnp.concatenate([jnp.ones((1,), jnp.int32), (tile_e[1:] != tile_e[:-1]).astype(jnp.int32)])
    tile_src = jnp.minimum(jnp.arange(n_tiles, dtype=jnp.int32), n_used - 1)
    plan = jnp.concatenate([pstart + counts, pend, n_used[None]]).astype(jnp.int32)
    return dest.reshape(T, TOP_K).astype(jnp.int32), plan, tile_e.astype(jnp.int32), tile_first, tile_valid, tile_src


def kernel(x_prompt, x_sample, cache_ckv, cache_krope, state_pool, page_table, meta_tokens, norm1_g, w_in, q_norm_g, w_uq, kv_norm_g, w_uk, w_uv, w_pool, pool_scale, w_out, norm2_g, w_router, b_router, w_gate_up, b_gate_up, w_down, b_down, final_norm_g):
    B, L, _ = x_prompt.shape
    n = x_sample.shape[0]
    n_pages = page_table.shape[1]
    past_len = n_pages * PAGE_SIZE
    wts = _prep_weights(norm1_g[0], w_in[0], q_norm_g[0], w_uq[0], kv_norm_g[0], w_uk[0], w_uv[0],
                        w_pool[0], pool_scale[0], w_out[0], norm2_g[0], w_router[0], b_router[0])

    cos_m, sin_m = _rope_tables(jnp.arange(N_META))
    _, kcat_m, ckv_m, kr_m, p_m, _, _ = _front(
        meta_tokens[None], cos_m, sin_m, jnp.zeros((16, POOL_WIDTH), F32), wts, N_META)

    cos_p, sin_p = _rope_tables(N_META + jnp.arange(L))
    q, kcat, ckv_p, kr_p, ptail, sga, gbb = _front(x_prompt, cos_p, sin_p, p_m[0], wts, FRONT_TM)
    kmeta = jnp.pad(kcat_m[0], ((0, LANES - N_META), (0, 0)))
    merged_p = _attn(q, kcat, kmeta, sga, gbb, wts, ATTN_T)

    cos_s, sin_s = _rope_tables(jnp.full((n,), past_len))
    hist_t = jnp.transpose(state_pool[0], (1, 0, 2))
    xs = x_sample[:, 0, :]
    qlat_t, q1_s, q2_s, ckv_s, kr_s, p_s, sga_s, gbb_s = _sfront(xs, cos_s, sin_s, hist_t, wts)
    qlat_s = jnp.transpose(qlat_t, (1, 0, 2))
    qrope_s = jnp.concatenate([q1_s.reshape(n, N_HEADS, ROPE_HALF), q2_s.reshape(n, N_HEADS, ROPE_HALF)], axis=2)
    eye = jnp.eye(ROPE_GROUP, dtype=F32)
    qexp_s = (eye[None, :, None, :, None] * qrope_s[:, None, :, None, :]).reshape(
        n, ROPE_GROUP * N_HEADS, ROPE_GROUP * ROPE_DIM).astype(BF16)
    o_s = _sattn(page_table, qlat_s, qexp_s, qrope_s, ckv_s[:, None, :], kr_s[:, None, :],
                 cache_ckv[0], jnp.swapaxes(cache_krope[0], 1, 2), min(SATTN_PAGES, n_pages))
    merged_s = _smerge(jnp.transpose(o_s, (1, 0, 2)), sga_s, gbb_s, wts)

    Tp = B * L
    n_pad = -(-n // POST_TM) * POST_TM
    x_all = jnp.concatenate([x_prompt.reshape(Tp, D_MODEL), xs, jnp.zeros((n_pad - n, D_MODEL), F32)], axis=0)
    mg_all = jnp.concatenate([merged_p.reshape(Tp, D_MODEL), merged_s,
                              jnp.zeros((n_pad - n, D_MODEL), BF16)], axis=0)
    x1, h2, ti, tg = _post(x_all, mg_all, wts, POST_TM)
    T = Tp + n
    dest, plan, tile_e, tile_first, tile_valid, tile_src = _route(ti[:T, 0:TOP_K], MOE_TM)
    cap = tile_e.shape[0] * MOE_TM
    xs_g = _dispatch(plan, dest, h2, cap, DISPATCH_TD, MOE_TM)
    yp = _experts(tile_e, tile_first, tile_valid, tile_src, xs_g, w_gate_up[0], b_gate_up[0], w_down[0],
                  b_down[0], MOE_TM)
    dest_pad = jnp.concatenate([dest, jnp.zeros((n_pad - n, TOP_K), jnp.int32)], axis=0)
    y = _combine(dest_pad, x1, tg, final_norm_g[None, :], yp, POST_TM)

    y_prompt = y[:Tp].reshape(B, L, D_MODEL)
    y_sample = y[Tp:T].reshape(n, 1, D_MODEL)
    new_ckv_prompt = jnp.concatenate([jnp.broadcast_to(ckv_m, (B, N_META, KV_RANK)), ckv_p], axis=1)[None]
    new_krope_prompt = jnp.concatenate([jnp.broadcast_to(kr_m, (B, N_META, ROPE_DIM)), kr_p], axis=1)[None]
    new_pool_prompt = ptail[:, 16 - POOL_HIST:, :][None]
    new_ckv_sample = ckv_s[None, :, None, :]
    new_krope_sample = kr_s[None, :, None, :]
    new_pool_sample = jnp.concatenate([state_pool[0][:, 1:, :], p_s[:, None, :]], axis=1)[None]
    return (y_prompt, y_sample, new_ckv_prompt, new_krope_prompt, new_pool_prompt,
            new_ckv_sample, new_krope_sample, new_pool_sample)
```

```python
import functools
import math

import jax
import jax.numpy as jnp
from jax import lax
from jax.experimental import pallas as pl
from jax.experimental.pallas import tpu as pltpu

F32 = jnp.float32
BF16 = jnp.bfloat16

D_MODEL = 1024
N_META = 16
N_HEADS = 8
Q_RANK = 256
KV_RANK = 256
NOPE_DIM = 64
ROPE_DIM = 32
ROPE_HALF = ROPE_DIM // 2
QK_DIM = NOPE_DIM + ROPE_DIM
V_DIM = D_MODEL // N_HEADS
ROPE_BASE = 10000.0
ATTN_SCALE = 1.0 / math.sqrt(QK_DIM)
Q_SCALE = ATTN_SCALE * math.log2(math.e)
POOL_WIDTH = D_MODEL // 2
POOL_WINDOWS = (2, 4, 8, 16)
POOL_GROUP = POOL_WIDTH // len(POOL_WINDOWS)
POOL_OUT_GROUP = D_MODEL // len(POOL_WINDOWS)
POOL_HIST = max(POOL_WINDOWS) - 1
N_EXPERTS = 32
TOP_K = 4
D_FF = D_MODEL
SWIGLU_LIMIT = 7.0
SWIGLU_ALPHA = 1.702
EPS = 1e-6
PAGE_SIZE = 128

LANES = 128
ROW_SUB = D_MODEL // LANES
NEG_BIG = -1e30

C_CQ = 0
C_CKV = C_CQ + Q_RANK
C_K1 = C_CKV + KV_RANK
C_K2 = C_K1 + LANES
C_P = C_K2 + LANES
C_GA = C_P + POOL_WIDTH
C_GB = C_GA + D_MODEL
C_END = C_GB + D_MODEL
Q_NOPE_W = N_HEADS * NOPE_DIM
Q_COLS = Q_NOPE_W + 2 * LANES
KCAT_W = KV_RANK + 2 * LANES

FRONT_TM = 256
ATTN_T = 256
POST_TM = 256
MOE_TM = 256
SATTN_PAGES = 64
ROPE_GROUP = 2 * LANES // ROPE_DIM
DISPATCH_TD = 128
VMEM_LIMIT = 56 * 1024 * 1024


def _rms(x, g):
    return x * lax.rsqrt(jnp.mean(x * x, axis=-1, keepdims=True) + EPS) * g


def _sigmoid(x):
    return 1.0 / (1.0 + jnp.exp(-x))


def _store_row_tiles(ref, x):
    n = x.shape[0]
    for s in range(ROW_SUB):
        ref[pl.ds(s, n, stride=ROW_SUB), :] = x[:, s * LANES:(s + 1) * LANES]


def _load_row_tiles(ref, n, first=0):
    return jnp.concatenate(
        [ref[pl.ds(first * ROW_SUB + s, n, stride=ROW_SUB), :] for s in range(ROW_SUB)], axis=1)


def _row_tile(ref, r):
    return ref.at[pl.ds(pl.multiple_of(r * ROW_SUB, ROW_SUB), ROW_SUB)]


def _rope_pair_to_rows(k1r, k2r):
    lane = lax.broadcasted_iota(jnp.int32, k1r.shape, 1)
    return jnp.where(lane < ROPE_HALF, k1r, pltpu.roll(k2r, ROPE_HALF, axis=1))[:, 0:ROPE_DIM]


def _front_kernel(x_ref, g1_ref, w_ref, qg_ref, wuq_ref, kvg_ref, cos_ref, sin_ref, wpool_ref,
                  pscale_ref, pprev_ref,
                  q_ref, kcat_ref, ckv_ref, kr_ref, ptail_ref, sga_ref, gbb_ref,
                  pext_ref, *, tm):
    i = pl.program_id(1)
    x = x_ref[0]
    hb = _rms(x, g1_ref[...]).astype(BF16)
    z = jnp.dot(hb, w_ref[...], preferred_element_type=F32)
    cos = cos_ref[...]
    sin = sin_ref[...]

    qn = _rms(z[:, C_CQ:C_CQ + Q_RANK], qg_ref[...]).astype(BF16)
    q = jnp.dot(qn, wuq_ref[...], preferred_element_type=F32) * Q_SCALE
    q1 = q[:, Q_NOPE_W:Q_NOPE_W + LANES]
    q2 = q[:, Q_NOPE_W + LANES:]
    q_ref[0, :, 0:Q_NOPE_W] = q[:, 0:Q_NOPE_W].astype(BF16)
    q_ref[0, :, Q_NOPE_W:Q_NOPE_W + LANES] = (q1 * cos - q2 * sin).astype(BF16)
    q_ref[0, :, Q_NOPE_W + LANES:] = (q1 * sin + q2 * cos).astype(BF16)

    ckv = _rms(z[:, C_CKV:C_CKV + KV_RANK], kvg_ref[...])
    ckv_ref[0] = ckv
    k1 = z[:, C_K1:C_K1 + LANES]
    k2 = z[:, C_K2:C_K2 + LANES]
    k1r = k1 * cos - k2 * sin
    k2r = k1 * sin + k2 * cos
    kcat_ref[0, :, 0:KV_RANK] = ckv.astype(BF16)
    kcat_ref[0, :, KV_RANK:KV_RANK + LANES] = k1r.astype(BF16)
    kcat_ref[0, :, KV_RANK + LANES:] = k2r.astype(BF16)
    kr_ref[0] = _rope_pair_to_rows(k1r, k2r)

    p = z[:, C_P:C_P + POOL_WIDTH]

    @pl.when(i == 0)
    def _():
        pext_ref[0:16, :] = pprev_ref[...]

    @pl.when(i > 0)
    def _():
        pext_ref[0:16, :] = pext_ref[tm:tm + 16, :]

    pext_ref[16:16 + tm, :] = p
    ptail_ref[0] = p[tm - 16:tm, :]
    bs = []
    for g, w in enumerate(POOL_WINDOWS):
        lo = g * POOL_GROUP
        acc = pext_ref[16:16 + tm, lo:lo + POOL_GROUP]
        for k in range(1, w):
            acc = acc + pext_ref[16 - k:16 - k + tm, lo:lo + POOL_GROUP]
        pooled = acc * (1.0 / w) - p[:, lo:lo + POOL_GROUP]
        bs.append(jnp.dot(pooled.astype(BF16), wpool_ref[g], preferred_element_type=F32))
    b = jnp.concatenate(bs, axis=1) * pscale_ref[...]
    sga_ref[0] = _sigmoid(z[:, C_GA:C_GA + D_MODEL]).astype(BF16)
    gbb_ref[0] = (_sigmoid(z[:, C_GB:C_GB + D_MODEL]) * b).astype(BF16)


def _front(x, cos_t, sin_t, pprev, wts, tm):
    B, L, _ = x.shape
    nt = L // tm
    full = lambda shape: pl.BlockSpec(shape, lambda b, i: (0,) * len(shape))
    row = lambda w: pl.BlockSpec((1, tm, w), lambda b, i: (b, i, 0))
    out_shape = (
        jax.ShapeDtypeStruct((B, L, Q_COLS), BF16),
        jax.ShapeDtypeStruct((B, L, KCAT_W), BF16),
        jax.ShapeDtypeStruct((B, L, KV_RANK), F32),
        jax.ShapeDtypeStruct((B, L, ROPE_DIM), F32),
        jax.ShapeDtypeStruct((B, 16, POOL_WIDTH), F32),
        jax.ShapeDtypeStruct((B, L, D_MODEL), BF16),
        jax.ShapeDtypeStruct((B, L, D_MODEL), BF16),
    )
    return pl.pallas_call(
        functools.partial(_front_kernel, tm=tm),
        out_shape=out_shape,
        grid=(B, nt),
        in_specs=[
            row(D_MODEL),
            full((1, D_MODEL)),
            full((D_MODEL, C_END)),
            full((1, Q_RANK)),
            full((Q_RANK, Q_COLS)),
            full((1, KV_RANK)),
            pl.BlockSpec((tm, LANES), lambda b, i: (i, 0)),
            pl.BlockSpec((tm, LANES), lambda b, i: (i, 0)),
            full((len(POOL_WINDOWS), POOL_GROUP, POOL_OUT_GROUP)),
            full((1, D_MODEL)),
            full((16, POOL_WIDTH)),
        ],
        out_specs=(
            row(Q_COLS), row(KCAT_W), row(KV_RANK), row(ROPE_DIM),
            pl.BlockSpec((1, 16, POOL_WIDTH), lambda b, i: (b, 0, 0)),
            row(D_MODEL), row(D_MODEL),
        ),
        scratch_shapes=[pltpu.VMEM((tm + 16, POOL_WIDTH), F32)],
        compiler_params=pltpu.CompilerParams(
            dimension_semantics=("arbitrary", "arbitrary"), vmem_limit_bytes=VMEM_LIMIT),
        name="front",
    )(x, wts["g1"], wts["w_in"], wts["qg"], wts["w_uq"], wts["kvg"], cos_t, sin_t, wts["w_pool"],
      wts["pool_scale"], pprev)


def _build_qcat(q, wukp_ref, qcat_ref, t):
    lane = lax.broadcasted_iota(jnp.int32, (t, LANES), 1)
    q1 = q[:, Q_NOPE_W:Q_NOPE_W + LANES].astype(F32)
    q2 = q[:, Q_NOPE_W + LANES:].astype(F32)
    zero = jnp.zeros_like(q1)
    for jp in range(N_HEADS // 2):
        ql = jnp.dot(q[:, jp * LANES:(jp + 1) * LANES], wukp_ref[jp], preferred_element_type=F32)
        for s in range(2):
            h = 2 * jp + s
            sel = (lane >= h * ROPE_HALF) & (lane < (h + 1) * ROPE_HALF)
            qcat_ref[h * t:(h + 1) * t, 0:KV_RANK] = ql[:, s * KV_RANK:(s + 1) * KV_RANK].astype(BF16)
            qcat_ref[h * t:(h + 1) * t, KV_RANK:KV_RANK + LANES] = jnp.where(sel, q1, zero).astype(BF16)
            qcat_ref[h * t:(h + 1) * t, KV_RANK + LANES:] = jnp.where(sel, q2, zero).astype(BF16)


def _attn_kernel(qi_ref, kj_ref, q_ref, kcat_ref, kmeta_ref, wukp_ref, wuv_ref, sga_ref, gbb_ref,
                 out_ref, qcat_ref, m_ref, l_ref, acc_ref, *, t):
    s_id = pl.program_id(1)
    i = qi_ref[s_id]
    j = kj_ref[s_id]
    rows = N_HEADS * t
    nt_dims = (((1,), (1,)), ((), ()))

    @pl.when(j == 0)
    def _():
        _build_qcat(q_ref[0], wukp_ref, qcat_ref, t)
        km = kmeta_ref[...]
        s = lax.dot_general(qcat_ref[...], km, nt_dims, preferred_element_type=F32)
        col = lax.broadcasted_iota(jnp.int32, (rows, LANES), 1)
        s = jnp.where(col < N_META, s, NEG_BIG)
        m = jnp.max(s, axis=1, keepdims=True)
        p = jnp.exp2(s - m)
        m_ref[...] = jnp.broadcast_to(m, (rows, LANES))
        l_ref[...] = jnp.broadcast_to(jnp.sum(p, axis=1, keepdims=True), (rows, LANES))
        acc_ref[...] = jnp.dot(p.astype(BF16), km[:, 0:KV_RANK], preferred_element_type=F32)

    def kv_tile_update(on_diagonal):
        k = kcat_ref[0]
        s = lax.dot_general(qcat_ref[...], k, nt_dims, preferred_element_type=F32)
        if on_diagonal:
            r_tok = lax.broadcasted_iota(jnp.int32, (rows, t), 0) & (t - 1)
            col = lax.broadcasted_iota(jnp.int32, (rows, t), 1)
            s = jnp.where(col <= r_tok, s, NEG_BIG)
        m_prev = m_ref[...]
        m_new = jnp.maximum(m_prev, jnp.max(s, axis=1, keepdims=True))
        alpha = jnp.exp2(m_prev - m_new)
        p = jnp.exp2(s - jnp.concatenate([m_new] * (t // LANES), axis=1))
        l_ref[...] = alpha * l_ref[...] + jnp.sum(p, axis=1, keepdims=True)
        m_ref[...] = m_new
        pv = jnp.dot(p.astype(BF16), k[:, 0:KV_RANK], preferred_element_type=F32)
        acc_ref[...] = acc_ref[...] * jnp.concatenate([alpha] * (KV_RANK // LANES), axis=1) + pv

    @pl.when(j < i)
    def _():
        kv_tile_update(False)

    @pl.when(j == i)
    def _():
        kv_tile_update(True)
        inv = 1.0 / l_ref[...]
        o = acc_ref[...] * jnp.concatenate([inv] * (KV_RANK // LANES), axis=1)
        a = jnp.concatenate(
            [jnp.dot(o[h * t:(h + 1) * t].astype(BF16), wuv_ref[h], preferred_element_type=F32)
             for h in range(N_HEADS)], axis=1)
        merged = sga_ref[0].astype(F32) * a + gbb_ref[0].astype(F32)
        out_ref[0] = merged.astype(BF16)


def _attn(q, kcat, kmeta, sga, gbb, wts, t):
    B, L, _ = q.shape
    nq = L // t
    pairs = [(i, j) for i in range(nq) for j in range(i + 1)]
    qi = jnp.asarray([p[0] for p in pairs], jnp.int32)
    kj = jnp.asarray([p[1] for p in pairs], jnp.int32)
    rows = N_HEADS * t
    grid_spec = pltpu.PrefetchScalarGridSpec(
        num_scalar_prefetch=2,
        grid=(B, len(pairs)),
        in_specs=[
            pl.BlockSpec((1, t, Q_COLS), lambda b, s, qi, kj: (b, qi[s], 0)),
            pl.BlockSpec((1, t, KCAT_W), lambda b, s, qi, kj: (b, kj[s], 0)),
            pl.BlockSpec((LANES, KCAT_W), lambda b, s, qi, kj: (0, 0)),
            pl.BlockSpec((N_HEADS // 2, LANES, 2 * KV_RANK), lambda b, s, qi, kj: (0, 0, 0)),
            pl.BlockSpec((N_HEADS, KV_RANK, V_DIM), lambda b, s, qi, kj: (0, 0, 0)),
            pl.BlockSpec((1, t, D_MODEL), lambda b, s, qi, kj: (b, qi[s], 0)),
            pl.BlockSpec((1, t, D_MODEL), lambda b, s, qi, kj: (b, qi[s], 0)),
        ],
        out_specs=pl.BlockSpec((1, t, D_MODEL), lambda b, s, qi, kj: (b, qi[s], 0)),
        scratch_shapes=[
            pltpu.VMEM((rows, KCAT_W), BF16),
            pltpu.VMEM((rows, LANES), F32),
            pltpu.VMEM((rows, LANES), F32),
            pltpu.VMEM((rows, KV_RANK), F32),
        ],
    )
    return pl.pallas_call(
        functools.partial(_attn_kernel, t=t),
        out_shape=jax.ShapeDtypeStruct((B, L, D_MODEL), BF16),
        grid_spec=grid_spec,
        compiler_params=pltpu.CompilerParams(
            dimension_semantics=("arbitrary", "arbitrary"), vmem_limit_bytes=VMEM_LIMIT),
        name="attn",
    )(qi, kj, q, kcat, kmeta, wts["w_ukp"], wts["w_uv"], sga, gbb)


def _sfront_kernel(x_ref, g1_ref, w_ref, qg_ref, wuq_ref, kvg_ref, cos_ref, sin_ref, wpool_ref,
                   pscale_ref, hist_ref, wukp_ref,
                   qlat_ref, q1_ref, q2_ref, ckv_ref, kr_ref, p_ref, sga_ref, gbb_ref):
    n = x_ref.shape[0]
    x = x_ref[...]
    hb = _rms(x, g1_ref[...]).astype(BF16)
    z = jnp.dot(hb, w_ref[...], preferred_element_type=F32)
    cos = cos_ref[...]
    sin = sin_ref[...]

    qn = _rms(z[:, C_CQ:C_CQ + Q_RANK], qg_ref[...]).astype(BF16)
    q = jnp.dot(qn, wuq_ref[...], preferred_element_type=F32) * Q_SCALE
    q1 = q[:, Q_NOPE_W:Q_NOPE_W + LANES]
    q2 = q[:, Q_NOPE_W + LANES:]
    q1_ref[...] = q1 * cos - q2 * sin
    q2_ref[...] = q1 * sin + q2 * cos
    qb = q[:, 0:Q_NOPE_W].astype(BF16)
    for jp in range(N_HEADS // 2):
        ql = jnp.dot(qb[:, jp * LANES:(jp + 1) * LANES], wukp_ref[jp], preferred_element_type=F32)
        qlat_ref[2 * jp] = ql[:, 0:KV_RANK].astype(BF16)
        qlat_ref[2 * jp + 1] = ql[:, KV_RANK:].astype(BF16)

    ckv = _rms(z[:, C_CKV:C_CKV + KV_RANK], kvg_ref[...])
    ckv_ref[...] = ckv
    k1 = z[:, C_K1:C_K1 + LANES]
    k2 = z[:, C_K2:C_K2 + LANES]
    k1r = k1 * cos - k2 * sin
    k2r = k1 * sin + k2 * cos
    kr_ref[...] = _rope_pair_to_rows(k1r, k2r)

    p = z[:, C_P:C_P + POOL_WIDTH]
    p_ref[...] = p
    bs = []
    for g, w in enumerate(POOL_WINDOWS):
        lo = g * POOL_GROUP
        acc = p[:, lo:lo + POOL_GROUP]
        for k in range(1, w):
            acc = acc + hist_ref[POOL_HIST - k, :, lo:lo + POOL_GROUP]
        pooled = acc * (1.0 / w) - p[:, lo:lo + POOL_GROUP]
        bs.append(jnp.dot(pooled.astype(BF16), wpool_ref[g], preferred_element_type=F32))
    b = jnp.concatenate(bs, axis=1) * pscale_ref[...]
    sga_ref[...] = _sigmoid(z[:, C_GA:C_GA + D_MODEL]).astype(BF16)
    gbb_ref[...] = (_sigmoid(z[:, C_GB:C_GB + D_MODEL]) * b).astype(BF16)


def _sfront(xs, cos_s, sin_s, hist_t, wts):
    n = xs.shape[0]
    out_shape = (
        jax.ShapeDtypeStruct((N_HEADS, n, KV_RANK), BF16),
        jax.ShapeDtypeStruct((n, LANES), F32),
        jax.ShapeDtypeStruct((n, LANES), F32),
        jax.ShapeDtypeStruct((n, KV_RANK), F32),
        jax.ShapeDtypeStruct((n, ROPE_DIM), F32),
        jax.ShapeDtypeStruct((n, POOL_WIDTH), F32),
        jax.ShapeDtypeStruct((n, D_MODEL), BF16),
        jax.ShapeDtypeStruct((n, D_MODEL), BF16),
    )
    return pl.pallas_call(
        _sfront_kernel,
        out_shape=out_shape,
        compiler_params=pltpu.CompilerParams(vmem_limit_bytes=VMEM_LIMIT),
        name="sfront",
    )(xs, wts["g1"], wts["w_in"], wts["qg"], wts["w_uq"], wts["kvg"], cos_s, sin_s, wts["w_pool"],
      wts["pool_scale"], hist_t, wts["w_ukp"])


def _sattn_kernel(pt_ref, qlat_ref, qexp_ref, qrope_ref, ckvn_ref, krn_ref, cckv_hbm, ckrt_hbm,
                  o_ref, kbuf, rbuf, sem, m_ref, l_ref, acc_ref, *, pc, nc):
    b = pl.program_id(0)
    c = pl.program_id(1)
    nb = pl.num_programs(0)
    step = b * nc + c
    slot = step % 2
    nt_dims = (((1,), (1,)), ((), ()))

    def start_pages(bb, cc, sl):
        base = bb * (nc * pc) + cc * pc

        def body(pg, carry):
            phys = pt_ref[base + pg]
            pltpu.make_async_copy(cckv_hbm.at[phys], kbuf.at[sl, pg], sem.at[0, sl]).start()
            pltpu.make_async_copy(ckrt_hbm.at[phys], rbuf.at[sl, pg], sem.at[1, sl]).start()
            return carry

        lax.fori_loop(0, pc, body, 0)

    @pl.when(step == 0)
    def _():
        start_pages(b, c, slot)

    nxt = step + 1

    @pl.when(nxt < nb * nc)
    def _():
        start_pages(nxt // nc, nxt % nc, 1 - slot)

    ql = qlat_ref[0]

    @pl.when(c == 0)
    def _():
        cn = ckvn_ref[0]
        s0 = (jnp.sum(ql.astype(F32) * cn, axis=1, keepdims=True)
              + jnp.sum(qrope_ref[0] * krn_ref[0], axis=1, keepdims=True))
        m_ref[...] = jnp.broadcast_to(s0, (N_HEADS, LANES))
        l_ref[...] = jnp.ones((N_HEADS, LANES), F32)
        acc_ref[...] = jnp.broadcast_to(cn, (N_HEADS, KV_RANK))

    pltpu.make_async_copy(cckv_hbm.at[pl.ds(0, pc)], kbuf.at[slot], sem.at[0, slot]).wait()
    pltpu.make_async_copy(ckrt_hbm.at[pl.ds(0, pc)], rbuf.at[slot], sem.at[1, slot]).wait()

    kb = kbuf[slot].reshape(pc * PAGE_SIZE, KV_RANK).astype(BF16)
    s_lat = lax.dot_general(ql, kb, nt_dims, preferred_element_type=F32)
    qexp = qexp_ref[0]
    pieces = []
    for g in range(pc // ROPE_GROUP):
        r8 = rbuf[slot, g * ROPE_GROUP:(g + 1) * ROPE_GROUP].reshape(ROPE_GROUP * ROPE_DIM, PAGE_SIZE)
        og = jnp.dot(qexp, r8.astype(BF16), preferred_element_type=F32)
        pieces += [og[pp * N_HEADS:(pp + 1) * N_HEADS] for pp in range(ROPE_GROUP)]
    s = s_lat + jnp.concatenate(pieces, axis=1)
    m_prev = m_ref[...]
    m_new = jnp.maximum(m_prev, jnp.max(s, axis=1, keepdims=True))
    alpha = jnp.exp2(m_prev - m_new)
    p = jnp.exp2(s - m_new[:, 0:1])
    l_ref[...] = alpha * l_ref[...] + jnp.sum(p, axis=1, keepdims=True)
    m_ref[...] = m_new
    pv = jnp.dot(p.astype(BF16), kb, preferred_element_type=F32)
    acc_ref[...] = acc_ref[...] * jnp.concatenate([alpha] * (KV_RANK // LANES), axis=1) + pv

    @pl.when(c == nc - 1)
    def _():
        inv = 1.0 / l_ref[...]
        o_ref[0] = acc_ref[...] * jnp.concatenate([inv] * (KV_RANK // LANES), axis=1)


def _sattn(page_table, qlat, qexp, qrope, ckv_new, kr_new, cache_ckv, cache_krt, pc):
    n, n_pages = page_table.shape
    nc = n_pages // pc
    grid_spec = pltpu.PrefetchScalarGridSpec(
        num_scalar_prefetch=1,
        grid=(n, nc),
        in_specs=[
            pl.BlockSpec((1, N_HEADS, KV_RANK), lambda b, c, pt: (b, 0, 0)),
            pl.BlockSpec((1, ROPE_GROUP * N_HEADS, ROPE_GROUP * ROPE_DIM), lambda b, c, pt: (b, 0, 0)),
            pl.BlockSpec((1, N_HEADS, ROPE_DIM), lambda b, c, pt: (b, 0, 0)),
            pl.BlockSpec((1, 1, KV_RANK), lambda b, c, pt: (b, 0, 0)),
            pl.BlockSpec((1, 1, ROPE_DIM), lambda b, c, pt: (b, 0, 0)),
            pl.BlockSpec(memory_space=pl.ANY),
            pl.BlockSpec(memory_space=pl.ANY),
        ],
        out_specs=pl.BlockSpec((1, N_HEADS, KV_RANK), lambda b, c, pt: (b, 0, 0)),
        scratch_shapes=[
            pltpu.VMEM((2, pc, PAGE_SIZE, KV_RANK), F32),
            pltpu.VMEM((2, pc, ROPE_DIM, PAGE_SIZE), F32),
            pltpu.SemaphoreType.DMA((2, 2)),
            pltpu.VMEM((N_HEADS, LANES), F32),
            pltpu.VMEM((N_HEADS, LANES), F32),
            pltpu.VMEM((N_HEADS, KV_RANK), F32),
        ],
    )
    return pl.pallas_call(
        functools.partial(_sattn_kernel, pc=pc, nc=nc),
        out_shape=jax.ShapeDtypeStruct((n, N_HEADS, KV_RANK), F32),
        grid_spec=grid_spec,
        compiler_params=pltpu.CompilerParams(
            dimension_semantics=("arbitrary", "arbitrary"), vmem_limit_bytes=VMEM_LIMIT),
        name="sattn",
    )(page_table.reshape(-1), qlat, qexp, qrope, ckv_new, kr_new, cache_ckv, cache_krt)


def _smerge_kernel(o_ref, wuv_ref, sga_ref, gbb_ref, out_ref):
    a = jnp.concatenate(
        [jnp.dot(o_ref[h].astype(BF16), wuv_ref[h], preferred_element_type=F32)
         for h in range(N_HEADS)], axis=1)
    out_ref[...] = (sga_ref[...].astype(F32) * a + gbb_ref[...].astype(F32)).astype(BF16)


def _smerge(o_t, sga, gbb, wts):
    n = sga.shape[0]
    return pl.pallas_call(
        _smerge_kernel,
        out_shape=jax.ShapeDtypeStruct((n, D_MODEL), BF16),
        name="smerge",
    )(o_t, wts["w_uv"], sga, gbb)


def _post_kernel(x_ref, mg_ref, wout_ref, g2_ref, wr_ref, br_ref, x1_ref, h2_ref, ti_ref, tg_ref):
    tm = x_ref.shape[0]
    x1 = x_ref[...] + jnp.dot(mg_ref[...], wout_ref[...], preferred_element_type=F32)
    x1_ref[...] = x1
    h2 = _rms(x1, g2_ref[...])
    _store_row_tiles(h2_ref, h2)
    hh = h2.astype(BF16)
    hl = (h2 - hh.astype(F32)).astype(BF16)
    o1 = jnp.dot(hh, wr_ref[...], preferred_element_type=F32)
    o2 = jnp.dot(hl, wr_ref[:, 0:LANES], preferred_element_type=F32)
    logits = o1[:, 0:LANES] + o1[:, LANES:] + o2 + br_ref[...]
    lane = lax.broadcasted_iota(jnp.int32, (tm, LANES), 1)
    vals = logits
    tops, idxs = [], []
    for _ in range(TOP_K):
        m = jnp.max(vals, axis=1, keepdims=True)
        idx = jnp.min(jnp.where(vals == m, lane, LANES), axis=1, keepdims=True)
        tops.append(m)
        idxs.append(idx)
        vals = jnp.where(lane == idx, NEG_BIG * 2, vals)
    es = [jnp.exp(v - tops[0]) for v in tops]
    den = es[0] + es[1] + es[2] + es[3]
    ti = jnp.zeros((tm, LANES), jnp.int32)
    tg = jnp.zeros((tm, LANES), F32)
    for k in range(TOP_K):
        ti = jnp.where(lane == k, idxs[k], ti)
        tg = jnp.where(lane == k, es[k] / den, tg)
    ti_ref[...] = ti
    tg_ref[...] = tg


def _post(x, merged, wts, tm):
    T = x.shape[0]
    full = lambda shape: pl.BlockSpec(shape, lambda i: (0,) * len(shape))
    row = lambda w: pl.BlockSpec((tm, w), lambda i: (i, 0))
    return pl.pallas_call(
        _post_kernel,
        out_shape=(
            jax.ShapeDtypeStruct((T, D_MODEL), F32),
            jax.ShapeDtypeStruct((T * ROW_SUB, LANES), F32),
            jax.ShapeDtypeStruct((T, LANES), jnp.int32),
            jax.ShapeDtypeStruct((T, LANES), F32),
        ),
        grid=(T // tm,),
        in_specs=[row(D_MODEL), row(D_MODEL), full((D_MODEL, D_MODEL)), full((1, D_MODEL)),
                  full((D_MODEL, 2 * LANES)), full((1, LANES))],
        out_specs=(row(D_MODEL), pl.BlockSpec((tm * ROW_SUB, LANES), lambda i: (i, 0)),
                   row(LANES), row(LANES)),
        compiler_params=pltpu.CompilerParams(
            dimension_semantics=("arbitrary",), vmem_limit_bytes=VMEM_LIMIT),
        name="post",
    )(x, merged, wts["w_out"], wts["g2"], wts["w_r"], wts["b_r"])


def _dispatch_kernel(plan_ref, dest_ref, hp_ref, hs_ref, xs_hbm, hbuf, sem, psem, *, td, ntp, n_slots):
    i = pl.program_id(0)
    n = pl.num_programs(0)
    slot = i % 2

    def wait_tile(s):
        span = pl.ds(0, td * TOP_K * ROW_SUB)
        pltpu.make_async_copy(xs_hbm.at[span], xs_hbm.at[span], sem.at[s]).wait()

    @pl.when(i >= 2)
    def _():
        wait_tile(slot)

    @pl.when(i < ntp)
    def _():
        hbuf[slot] = hp_ref[...]

    @pl.when(i >= ntp)
    def _():
        hbuf[slot] = hs_ref[...]

    def body(r8, c):
        for rr in range(8):
            r = r8 * 8 + rr
            for k in range(TOP_K):
                d = dest_ref[0, 0, r * TOP_K + k]
                pltpu.make_async_copy(_row_tile(hbuf.at[slot], r), _row_tile(xs_hbm, d),
                                      sem.at[slot]).start(priority=k % 2)
        return c

    lax.fori_loop(0, td // 8, body, 0)

    @pl.when(i == n - 1)
    def _():
        def fill(s, c):
            pltpu.make_async_copy(_row_tile(hbuf.at[slot], 0), _row_tile(xs_hbm, s), psem.at[0]).start()
            return c

        def drain(s, c):
            pltpu.make_async_copy(_row_tile(hbuf.at[slot], 0), _row_tile(xs_hbm, s), psem.at[0]).wait()
            return c

        def per_expert(e, c):
            lo = plan_ref[e]
            hi = plan_ref[N_EXPERTS + e]
            lax.fori_loop(lo, hi, fill, 0)
            lax.fori_loop(lo, hi, drain, 0)
            return c

        lax.fori_loop(0, N_EXPERTS, per_expert, 0)
        lax.fori_loop(plan_ref[2 * N_EXPERTS - 1], n_slots, fill, 0)
        lax.fori_loop(plan_ref[2 * N_EXPERTS - 1], n_slots, drain, 0)
        wait_tile(slot)

        @pl.when(n >= 2)
        def _():
            wait_tile(1 - slot)


def _dispatch(plan, dest, h2_p, h2_s, cap, td):
    T = dest.shape[0]
    nt = T // td
    ntp = h2_p.shape[0] // (td * ROW_SUB)
    tile_spec = lambda f: pl.BlockSpec((td * ROW_SUB, LANES), f)
    grid_spec = pltpu.PrefetchScalarGridSpec(
        num_scalar_prefetch=1,
        grid=(nt,),
        in_specs=[
            pl.BlockSpec((1, 1, td * TOP_K), lambda i, plan: (i, 0, 0), memory_space=pltpu.SMEM),
            tile_spec(lambda i, plan: (jnp.minimum(i, ntp - 1), 0)),
            tile_spec(lambda i, plan: (jnp.maximum(i - ntp, 0), 0)),
        ],
        out_specs=pl.BlockSpec(memory_space=pl.ANY),
        scratch_shapes=[pltpu.VMEM((2, td * ROW_SUB, LANES), F32), pltpu.SemaphoreType.DMA((2,)),
                        pltpu.SemaphoreType.DMA((1,))],
    )
    return pl.pallas_call(
        functools.partial(_dispatch_kernel, td=td, ntp=ntp, n_slots=cap),
        out_shape=jax.ShapeDtypeStruct((cap * ROW_SUB, LANES), F32),
        grid_spec=grid_spec,
        compiler_params=pltpu.CompilerParams(dimension_semantics=("arbitrary",)),
        name="dispatch",
    )(plan, dest.reshape(nt, 1, td * TOP_K), h2_p, h2_s)


def _experts_kernel(te_ref, tf_ref, tv_ref, ts_ref, xs_ref, wgu_ref, bgu_ref, wdn_ref, bdn_ref, out_ref,
                    wgu_bf, wdn_bf):
    t = pl.program_id(0)

    @pl.when(tf_ref[t] == 1)
    def _():
        wgu_bf[...] = wgu_ref[0].astype(BF16)
        wdn_bf[...] = wdn_ref[0].astype(BF16)

    @pl.when(tv_ref[t] == 1)
    def _():
        x = _load_row_tiles(xs_ref, xs_ref.shape[0] // ROW_SUB).astype(BF16)
        gu = jnp.dot(x, wgu_bf[...], preferred_element_type=F32) + bgu_ref[0]
        gate = jnp.minimum(gu[:, 0:D_FF], SWIGLU_LIMIT)
        up = jnp.clip(gu[:, D_FF:], -SWIGLU_LIMIT, SWIGLU_LIMIT)
        act = gate * _sigmoid(gate * SWIGLU_ALPHA)
        hmid = ((up + 1.0) * act).astype(BF16)
        _store_row_tiles(out_ref, jnp.dot(hmid, wdn_bf[...], preferred_element_type=F32) + bdn_ref[0])

    @pl.when(tv_ref[t] == 0)
    def _():
        out_ref[...] = jnp.zeros_like(out_ref)


def _experts(tile_e, tile_first, tile_valid, tile_src, xs, w_gate_up, b_gate_up, w_down, b_down, tm):
    cap = xs.shape[0] // ROW_SUB
    n_tiles = cap // tm
    grid_spec = pltpu.PrefetchScalarGridSpec(
        num_scalar_prefetch=4,
        grid=(n_tiles,),
        in_specs=[
            pl.BlockSpec((tm * ROW_SUB, LANES), lambda t, te, tf, tv, ts: (ts[t], 0)),
            pl.BlockSpec((1, D_MODEL, 2 * D_FF), lambda t, te, tf, tv, ts: (te[t], 0, 0)),
            pl.BlockSpec((1, 1, 2 * D_FF), lambda t, te, tf, tv, ts: (te[t], 0, 0)),
            pl.BlockSpec((1, D_FF, D_MODEL), lambda t, te, tf, tv, ts: (te[t], 0, 0)),
            pl.BlockSpec((1, 1, D_MODEL), lambda t, te, tf, tv, ts: (te[t], 0, 0)),
        ],
        out_specs=pl.BlockSpec((tm * ROW_SUB, LANES), lambda t, te, tf, tv, ts: (t, 0)),
        scratch_shapes=[pltpu.VMEM((D_MODEL, 2 * D_FF), BF16), pltpu.VMEM((D_FF, D_MODEL), BF16)],
    )
    return pl.pallas_call(
        _experts_kernel,
        out_shape=jax.ShapeDtypeStruct((cap * ROW_SUB, LANES), F32),
        grid_spec=grid_spec,
        compiler_params=pltpu.CompilerParams(
            dimension_semantics=("arbitrary",), vmem_limit_bytes=VMEM_LIMIT),
        name="experts",
    )(tile_e, tile_first, tile_valid, tile_src, xs, w_gate_up, b_gate_up.reshape(N_EXPERTS, 1, 2 * D_FF),
      w_down, b_down.reshape(N_EXPERTS, 1, D_MODEL))


def _combine_kernel(dcur_ref, dnxt_ref, x1_ref, tg_ref, gf_ref, yp_hbm, out_ref, buf, sem, *, tm):
    i = pl.program_id(0)
    n = pl.num_programs(0)
    slot = i % 2

    def start_rows(dref, sl):
        def body(r8, c):
            for rr in range(8):
                r = r8 * 8 + rr
                for k in range(TOP_K):
                    d = dref[0, 0, r * TOP_K + k]
                    pltpu.make_async_copy(_row_tile(yp_hbm, d), _row_tile(buf.at[sl], k * tm + r),
                                          sem.at[sl]).start(priority=k % 2)
            return c

        lax.fori_loop(0, tm // 8, body, 0)

    @pl.when(i == 0)
    def _():
        start_rows(dcur_ref, slot)

    @pl.when(i + 1 < n)
    def _():
        start_rows(dnxt_ref, 1 - slot)

    pltpu.make_async_copy(yp_hbm.at[pl.ds(0, TOP_K * tm * ROW_SUB)], buf.at[slot], sem.at[slot]).wait()
    y = x1_ref[...]
    tg = tg_ref[...]
    for k in range(TOP_K):
        y = y + _load_row_tiles(buf.at[slot], tm, first=k * tm) * tg[:, k:k + 1]
    out_ref[...] = _rms(y, gf_ref[...])


def _combine(dest, x1, tg, gf, yp, tm):
    T = x1.shape[0]
    nt = T // tm
    dest3 = dest.reshape(nt, 1, tm * TOP_K)
    return pl.pallas_call(
        functools.partial(_combine_kernel, tm=tm),
        out_shape=jax.ShapeDtypeStruct((T, D_MODEL), F32),
        grid=(nt,),
        in_specs=[
            pl.BlockSpec((1, 1, tm * TOP_K), lambda i: (i, 0, 0), memory_space=pltpu.SMEM),
            pl.BlockSpec((1, 1, tm * TOP_K), lambda i: (jnp.minimum(i + 1, nt - 1), 0, 0),
                         memory_space=pltpu.SMEM),
            pl.BlockSpec((tm, D_MODEL), lambda i: (i, 0)),
            pl.BlockSpec((tm, LANES), lambda i: (i, 0)),
            pl.BlockSpec((1, D_MODEL), lambda i: (0, 0)),
            pl.BlockSpec(memory_space=pl.ANY),
        ],
        out_specs=pl.BlockSpec((tm, D_MODEL), lambda i: (i, 0)),
        scratch_shapes=[pltpu.VMEM((2, TOP_K * tm * ROW_SUB, LANES), F32), pltpu.SemaphoreType.DMA((2,))],
        compiler_params=pltpu.CompilerParams(
            dimension_semantics=("arbitrary",), vmem_limit_bytes=VMEM_LIMIT),
        name="combine",
    )(dest3, dest3, x1, tg, gf, yp)


def _rope_tables(pos):
    inv = ROPE_BASE ** (-jnp.arange(ROPE_HALF, dtype=F32) / ROPE_HALF)
    ang = pos.astype(F32)[:, None] * inv[None, :]
    return jnp.tile(jnp.cos(ang), (1, N_HEADS)), jnp.tile(jnp.sin(ang), (1, N_HEADS))


def _prep_weights(norm1_g, w_in, q_norm_g, w_uq, kv_norm_g, w_uk, w_uv, w_pool, pool_scale, w_out,
                  norm2_g, w_router, b_router):
    i0 = Q_RANK
    i1 = i0 + KV_RANK
    i2 = i1 + ROPE_DIM
    i3 = i2 + POOL_WIDTH
    w_packed = jnp.concatenate([
        w_in[:, 0:i1],
        jnp.tile(w_in[:, i1:i1 + ROPE_HALF], (1, N_HEADS)),
        jnp.tile(w_in[:, i1 + ROPE_HALF:i2], (1, N_HEADS)),
        w_in[:, i2:],
    ], axis=1).astype(BF16)
    wq = w_uq.reshape(Q_RANK, N_HEADS, QK_DIM)
    wq_packed = jnp.concatenate([
        wq[:, :, 0:NOPE_DIM].reshape(Q_RANK, Q_NOPE_W),
        wq[:, :, NOPE_DIM:NOPE_DIM + ROPE_HALF].reshape(Q_RANK, LANES),
        wq[:, :, NOPE_DIM + ROPE_HALF:].reshape(Q_RANK, LANES),
    ], axis=1).astype(BF16)
    zeros = jnp.zeros((NOPE_DIM, KV_RANK), F32)
    w_ukp = jnp.stack([
        jnp.concatenate([
            jnp.concatenate([w_uk[2 * jp], zeros], axis=1),
            jnp.concatenate([zeros, w_uk[2 * jp + 1]], axis=1)], axis=0)
        for jp in range(N_HEADS // 2)]).astype(BF16)
    wr_hi = w_router.astype(BF16)
    wr_lo = (w_router - wr_hi.astype(F32)).astype(BF16)
    pad = ((0, 0), (0, LANES - N_EXPERTS))
    w_r = jnp.concatenate([jnp.pad(wr_hi, pad), jnp.pad(wr_lo, pad)], axis=1)
    b_r = jnp.concatenate([b_router.astype(F32), jnp.full((LANES - N_EXPERTS,), NEG_BIG, F32)])[None, :]
    return {
        "g1": norm1_g[None, :], "w_in": w_packed, "qg": q_norm_g[None, :], "w_uq": wq_packed,
        "kvg": kv_norm_g[None, :], "w_ukp": w_ukp, "w_uv": w_uv.astype(BF16),
        "w_pool": w_pool.astype(BF16), "pool_scale": pool_scale[None, :], "w_out": w_out.astype(BF16),
        "g2": norm2_g[None, :], "w_r": w_r, "b_r": b_r,
    }


def _route(top_i, tm):
    T = top_i.shape[0]
    n_asg = T * TOP_K
    n_tiles = -(-n_asg // tm) + N_EXPERTS
    e = top_i.reshape(-1)
    onehot = (e[:, None] == jnp.arange(N_EXPERTS, dtype=jnp.int32)[None, :]).astype(jnp.int32)
    csum = jnp.cumsum(onehot, axis=0)
    counts = csum[-1]
    padded = (counts + tm - 1) // tm * tm
    pend = jnp.cumsum(padded)
    pstart = pend - padded
    dest = jnp.sum(onehot * (csum - 1 + pstart[None, :]), axis=1)
    tile_start = jnp.arange(n_tiles, dtype=jnp.int32) * tm
    tile_e = jnp.minimum(jnp.sum((pend[None, :] <= tile_start[:, None]).astype(jnp.int32), axis=1), N_EXPERTS - 1)
    n_used = pend[-1] // tm
    tile_valid = (tile_start < pend[-1]).astype(jnp.int32)
    tile_first = jnp.concatenate([jnp.ones((1,), jnp.int32), (tile_e[1:] != tile_e[:-1]).astype(jnp.int32)])
    tile_src = jnp.minimum(jnp.arange(n_tiles, dtype=jnp.int32), n_used - 1)
    plan = jnp.concatenate([pstart + counts, pend]).astype(jnp.int32)
    return dest.reshape(T, TOP_K).astype(jnp.int32), plan, tile_e.astype(jnp.int32), tile_first, tile_valid, tile_src


def kernel(x_prompt, x_sample, cache_ckv, cache_krope, state_pool, page_table, meta_tokens, norm1_g, w_in, q_norm_g, w_uq, kv_norm_g, w_uk, w_uv, w_pool, pool_scale, w_out, norm2_g, w_router, b_router, w_gate_up, b_gate_up, w_down, b_down, final_norm_g):
    B, L, _ = x_prompt.shape
    n = x_sample.shape[0]
    n_pages = page_table.shape[1]
    past_len = n_pages * PAGE_SIZE
    wts = _prep_weights(norm1_g[0], w_in[0], q_norm_g[0], w_uq[0], kv_norm_g[0], w_uk[0], w_uv[0],
                        w_pool[0], pool_scale[0], w_out[0], norm2_g[0], w_router[0], b_router[0])

    cos_m, sin_m = _rope_tables(jnp.arange(N_META))
    _, kcat_m, ckv_m, kr_m, p_m, _, _ = _front(
        meta_tokens[None], cos_m, sin_m, jnp.zeros((16, POOL_WIDTH), F32), wts, N_META)

    cos_p, sin_p = _rope_tables(N_META + jnp.arange(L))
    q, kcat, ckv_p, kr_p, ptail, sga, gbb = _front(x_prompt, cos_p, sin_p, p_m[0], wts, FRONT_TM)
    kmeta = jnp.pad(kcat_m[0], ((0, LANES - N_META), (0, 0)))
    merged_p = _attn(q, kcat, kmeta, sga, gbb, wts, ATTN_T)

    cos_s, sin_s = _rope_tables(jnp.full((n,), past_len))
    hist_t = jnp.transpose(state_pool[0], (1, 0, 2))
    xs = x_sample[:, 0, :]
    qlat_t, q1_s, q2_s, ckv_s, kr_s, p_s, sga_s, gbb_s = _sfront(xs, cos_s, sin_s, hist_t, wts)
    qlat_s = jnp.transpose(qlat_t, (1, 0, 2))
    qrope_s = jnp.concatenate([q1_s.reshape(n, N_HEADS, ROPE_HALF), q2_s.reshape(n, N_HEADS, ROPE_HALF)], axis=2)
    eye = jnp.eye(ROPE_GROUP, dtype=F32)
    qexp_s = (eye[None, :, None, :, None] * qrope_s[:, None, :, None, :]).reshape(
        n, ROPE_GROUP * N_HEADS, ROPE_GROUP * ROPE_DIM).astype(BF16)
    o_s = _sattn(page_table, qlat_s, qexp_s, qrope_s, ckv_s[:, None, :], kr_s[:, None, :],
                 cache_ckv[0], jnp.swapaxes(cache_krope[0], 1, 2), min(SATTN_PAGES, n_pages))
    merged_s = _smerge(jnp.transpose(o_s, (1, 0, 2)), sga_s, gbb_s, wts)

    Tp = B * L
    x1_p, h2_p, ti_p, tg_p = _post(x_prompt.reshape(Tp, D_MODEL), merged_p.reshape(Tp, D_MODEL), wts, POST_TM)
    x1_s, h2_s, ti_s, tg_s = _post(xs, merged_s, wts, n)
    top_i = jnp.concatenate([ti_p[:, 0:TOP_K], ti_s[:, 0:TOP_K]], axis=0)
    dest, plan, tile_e, tile_first, tile_valid, tile_src = _route(top_i, MOE_TM)
    cap = tile_e.shape[0] * MOE_TM
    xs_g = _dispatch(plan, dest, h2_p, h2_s, cap, DISPATCH_TD)
    yp = _experts(tile_e, tile_first, tile_valid, tile_src, xs_g, w_gate_up[0], b_gate_up[0], w_down[0],
                  b_down[0], MOE_TM)
    gf = final_norm_g[None, :]
    y_prompt = _combine(dest[:Tp], x1_p, tg_p, gf, yp, POST_TM).reshape(B, L, D_MODEL)
    y_sample = _combine(dest[Tp:], x1_s, tg_s, gf, yp, n).reshape(n, 1, D_MODEL)
    new_ckv_prompt = jnp.concatenate([jnp.broadcast_to(ckv_m, (B, N_META, KV_RANK)), ckv_p], axis=1)[None]
    new_krope_prompt = jnp.concatenate([jnp.broadcast_to(kr_m, (B, N_META, ROPE_DIM)), kr_p], axis=1)[None]
    new_pool_prompt = ptail[:, 16 - POOL_HIST:, :][None]
    new_ckv_sample = ckv_s[None, :, None, :]
    new_krope_sample = kr_s[None, :, None, :]
    new_pool_sample = jnp.concatenate([state_pool[0][:, 1:, :], p_s[:, None, :]], axis=1)[None]
    return (y_prompt, y_sample, new_ckv_prompt, new_krope_prompt, new_pool_prompt,
            new_ckv_sample, new_krope_sample, new_pool_sample)
```

```python
import functools
import math

import jax
import jax.numpy as jnp
from jax import lax
from jax.experimental import pallas as pl
from jax.experimental.pallas import tpu as pltpu

F32 = jnp.float32
BF16 = jnp.bfloat16

D_MODEL = 1024
N_META = 16
N_HEADS = 8
Q_RANK = 256
KV_RANK = 256
NOPE_DIM = 64
ROPE_DIM = 32
ROPE_HALF = ROPE_DIM // 2
QK_DIM = NOPE_DIM + ROPE_DIM
V_DIM = D_MODEL // N_HEADS
ROPE_BASE = 10000.0
ATTN_SCALE = 1.0 / math.sqrt(QK_DIM)
Q_SCALE = ATTN_SCALE * math.log2(math.e)
POOL_WIDTH = D_MODEL // 2
POOL_WINDOWS = (2, 4, 8, 16)
POOL_GROUP = POOL_WIDTH // len(POOL_WINDOWS)
POOL_OUT_GROUP = D_MODEL // len(POOL_WINDOWS)
POOL_HIST = max(POOL_WINDOWS) - 1
N_EXPERTS = 32
TOP_K = 4
D_FF = D_MODEL
SWIGLU_LIMIT = 7.0
SWIGLU_ALPHA = 1.702
EPS = 1e-6
PAGE_SIZE = 128

LANES = 128
ROW_SUB = D_MODEL // LANES
NEG_BIG = -1e30

C_CQ = 0
C_CKV = C_CQ + Q_RANK
C_K1 = C_CKV + KV_RANK
C_K2 = C_K1 + LANES
C_P = C_K2 + LANES
C_GA = C_P + POOL_WIDTH
C_GB = C_GA + D_MODEL
C_END = C_GB + D_MODEL
Q_NOPE_W = N_HEADS * NOPE_DIM
Q_COLS = Q_NOPE_W + 2 * LANES
KCAT_W = KV_RANK + 2 * LANES

FRONT_TM = 512
ATTN_T = 256
POST_TM = 512
MOE_TM = 512
SATTN_PAGES = 128
ROPE_GROUP = 2 * LANES // ROPE_DIM
DISPATCH_TD = 128
VMEM_LIMIT = 56 * 1024 * 1024


def _rms(x, g):
    return x * lax.rsqrt(jnp.mean(x * x, axis=-1, keepdims=True) + EPS) * g


def _sigmoid(x):
    return 1.0 / (1.0 + jnp.exp(-x))


def _store_row_tiles(ref, x):
    n = x.shape[0]
    for s in range(ROW_SUB):
        ref[pl.ds(s, n, stride=ROW_SUB), :] = x[:, s * LANES:(s + 1) * LANES]


def _load_row_tiles(ref, n, first=0):
    return jnp.concatenate(
        [ref[pl.ds(first * ROW_SUB + s, n, stride=ROW_SUB), :] for s in range(ROW_SUB)], axis=1)


def _row_tile(ref, r):
    return ref.at[pl.ds(pl.multiple_of(r * ROW_SUB, ROW_SUB), ROW_SUB)]


def _rope_pair_to_rows(k1r, k2r):
    lane = lax.broadcasted_iota(jnp.int32, k1r.shape, 1)
    return jnp.where(lane < ROPE_HALF, k1r, pltpu.roll(k2r, ROPE_HALF, axis=1))[:, 0:ROPE_DIM]


def _front_kernel(x_ref, g1_ref, w_ref, qg_ref, wuq_ref, kvg_ref, cos_ref, sin_ref, wpool_ref,
                  pscale_ref, pprev_ref,
                  q_ref, kcat_ref, ckv_ref, kr_ref, ptail_ref, sga_ref, gbb_ref,
                  pext_ref, *, tm):
    i = pl.program_id(1)
    x = x_ref[0]
    hb = _rms(x, g1_ref[...]).astype(BF16)
    z = jnp.dot(hb, w_ref[...], preferred_element_type=F32)
    cos = cos_ref[...]
    sin = sin_ref[...]

    qn = _rms(z[:, C_CQ:C_CQ + Q_RANK], qg_ref[...]).astype(BF16)
    q = jnp.dot(qn, wuq_ref[...], preferred_element_type=F32) * Q_SCALE
    q1 = q[:, Q_NOPE_W:Q_NOPE_W + LANES]
    q2 = q[:, Q_NOPE_W + LANES:]
    q_ref[0, :, 0:Q_NOPE_W] = q[:, 0:Q_NOPE_W].astype(BF16)
    q_ref[0, :, Q_NOPE_W:Q_NOPE_W + LANES] = (q1 * cos - q2 * sin).astype(BF16)
    q_ref[0, :, Q_NOPE_W + LANES:] = (q1 * sin + q2 * cos).astype(BF16)

    ckv = _rms(z[:, C_CKV:C_CKV + KV_RANK], kvg_ref[...])
    ckv_ref[0] = ckv
    k1 = z[:, C_K1:C_K1 + LANES]
    k2 = z[:, C_K2:C_K2 + LANES]
    k1r = k1 * cos - k2 * sin
    k2r = k1 * sin + k2 * cos
    kcat_ref[0, :, 0:KV_RANK] = ckv.astype(BF16)
    kcat_ref[0, :, KV_RANK:KV_RANK + LANES] = k1r.astype(BF16)
    kcat_ref[0, :, KV_RANK + LANES:] = k2r.astype(BF16)
    kr_ref[0] = _rope_pair_to_rows(k1r, k2r)

    p = z[:, C_P:C_P + POOL_WIDTH]

    @pl.when(i == 0)
    def _():
        pext_ref[0:16, :] = pprev_ref[...]

    @pl.when(i > 0)
    def _():
        pext_ref[0:16, :] = pext_ref[tm:tm + 16, :]

    pext_ref[16:16 + tm, :] = p
    ptail_ref[0] = p[tm - 16:tm, :]
    bs = []
    for g, w in enumerate(POOL_WINDOWS):
        lo = g * POOL_GROUP
        acc = pext_ref[16:16 + tm, lo:lo + POOL_GROUP]
        for k in range(1, w):
            acc = acc + pext_ref[16 - k:16 - k + tm, lo:lo + POOL_GROUP]
        pooled = acc * (1.0 / w) - p[:, lo:lo + POOL_GROUP]
        bs.append(jnp.dot(pooled.astype(BF16), wpool_ref[g], preferred_element_type=F32))
    b = jnp.concatenate(bs, axis=1) * pscale_ref[...]
    sga_ref[0] = _sigmoid(z[:, C_GA:C_GA + D_MODEL]).astype(BF16)
    gbb_ref[0] = (_sigmoid(z[:, C_GB:C_GB + D_MODEL]) * b).astype(BF16)


def _front(x, cos_t, sin_t, pprev, wts, tm):
    B, L, _ = x.shape
    nt = L // tm
    full = lambda shape: pl.BlockSpec(shape, lambda b, i: (0,) * len(shape))
    row = lambda w: pl.BlockSpec((1, tm, w), lambda b, i: (b, i, 0))
    out_shape = (
        jax.ShapeDtypeStruct((B, L, Q_COLS), BF16),
        jax.ShapeDtypeStruct((B, L, KCAT_W), BF16),
        jax.ShapeDtypeStruct((B, L, KV_RANK), F32),
        jax.ShapeDtypeStruct((B, L, ROPE_DIM), F32),
        jax.ShapeDtypeStruct((B, 16, POOL_WIDTH), F32),
        jax.ShapeDtypeStruct((B, L, D_MODEL), BF16),
        jax.ShapeDtypeStruct((B, L, D_MODEL), BF16),
    )
    return pl.pallas_call(
        functools.partial(_front_kernel, tm=tm),
        out_shape=out_shape,
        grid=(B, nt),
        in_specs=[
            row(D_MODEL),
            full((1, D_MODEL)),
            full((D_MODEL, C_END)),
            full((1, Q_RANK)),
            full((Q_RANK, Q_COLS)),
            full((1, KV_RANK)),
            pl.BlockSpec((tm, LANES), lambda b, i: (i, 0)),
            pl.BlockSpec((tm, LANES), lambda b, i: (i, 0)),
            full((len(POOL_WINDOWS), POOL_GROUP, POOL_OUT_GROUP)),
            full((1, D_MODEL)),
            full((16, POOL_WIDTH)),
        ],
        out_specs=(
            row(Q_COLS), row(KCAT_W), row(KV_RANK), row(ROPE_DIM),
            pl.BlockSpec((1, 16, POOL_WIDTH), lambda b, i: (b, 0, 0)),
            row(D_MODEL), row(D_MODEL),
        ),
        scratch_shapes=[pltpu.VMEM((tm + 16, POOL_WIDTH), F32)],
        compiler_params=pltpu.CompilerParams(
            dimension_semantics=("arbitrary", "arbitrary"), vmem_limit_bytes=VMEM_LIMIT),
        name="front",
    )(x, wts["g1"], wts["w_in"], wts["qg"], wts["w_uq"], wts["kvg"], cos_t, sin_t, wts["w_pool"],
      wts["pool_scale"], pprev)


def _build_qcat(q, wukp_ref, qcat_ref, t):
    lane = lax.broadcasted_iota(jnp.int32, (t, LANES), 1)
    q1 = q[:, Q_NOPE_W:Q_NOPE_W + LANES].astype(F32)
    q2 = q[:, Q_NOPE_W + LANES:].astype(F32)
    zero = jnp.zeros_like(q1)
    for jp in range(N_HEADS // 2):
        ql = jnp.dot(q[:, jp * LANES:(jp + 1) * LANES], wukp_ref[jp], preferred_element_type=F32)
        for s in range(2):
            h = 2 * jp + s
            sel = (lane >= h * ROPE_HALF) & (lane < (h + 1) * ROPE_HALF)
            qcat_ref[h * t:(h + 1) * t, 0:KV_RANK] = ql[:, s * KV_RANK:(s + 1) * KV_RANK].astype(BF16)
            qcat_ref[h * t:(h + 1) * t, KV_RANK:KV_RANK + LANES] = jnp.where(sel, q1, zero).astype(BF16)
            qcat_ref[h * t:(h + 1) * t, KV_RANK + LANES:] = jnp.where(sel, q2, zero).astype(BF16)


def _attn_kernel(qi_ref, kj_ref, q_ref, kcat_ref, kmeta_ref, wukp_ref, wuv_ref, sga_ref, gbb_ref,
                 out_ref, qcat_ref, m_ref, l_ref, acc_ref, *, t):
    s_id = pl.program_id(1)
    i = qi_ref[s_id]
    j = kj_ref[s_id]
    rows = N_HEADS * t
    nt_dims = (((1,), (1,)), ((), ()))

    @pl.when(j == 0)
    def _():
        _build_qcat(q_ref[0], wukp_ref, qcat_ref, t)
        km = kmeta_ref[...]
        s = lax.dot_general(qcat_ref[...], km, nt_dims, preferred_element_type=F32)
        col = lax.broadcasted_iota(jnp.int32, (rows, LANES), 1)
        s = jnp.where(col < N_META, s, NEG_BIG)
        m = jnp.max(s, axis=1, keepdims=True)
        p = jnp.exp2(s - m)
        m_ref[...] = jnp.broadcast_to(m, (rows, LANES))
        l_ref[...] = jnp.broadcast_to(jnp.sum(p, axis=1, keepdims=True), (rows, LANES))
        acc_ref[...] = jnp.dot(p.astype(BF16), km[:, 0:KV_RANK], preferred_element_type=F32)

    def kv_tile_update(on_diagonal):
        k = kcat_ref[0]
        s = lax.dot_general(qcat_ref[...], k, nt_dims, preferred_element_type=F32)
        if on_diagonal:
            r_tok = lax.broadcasted_iota(jnp.int32, (rows, t), 0) & (t - 1)
            col = lax.broadcasted_iota(jnp.int32, (rows, t), 1)
            s = jnp.where(col <= r_tok, s, NEG_BIG)
        m_prev = m_ref[...]
        m_new = jnp.maximum(m_prev, jnp.max(s, axis=1, keepdims=True))
        alpha = jnp.exp2(m_prev - m_new)
        p = jnp.exp2(s - jnp.concatenate([m_new] * (t // LANES), axis=1))
        l_ref[...] = alpha * l_ref[...] + jnp.sum(p, axis=1, keepdims=True)
        m_ref[...] = m_new
        pv = jnp.dot(p.astype(BF16), k[:, 0:KV_RANK], preferred_element_type=F32)
        acc_ref[...] = acc_ref[...] * jnp.concatenate([alpha] * (KV_RANK // LANES), axis=1) + pv

    @pl.when(j < i)
    def _():
        kv_tile_update(False)

    @pl.when(j == i)
    def _():
        kv_tile_update(True)
        inv = 1.0 / l_ref[...]
        o = acc_ref[...] * jnp.concatenate([inv] * (KV_RANK // LANES), axis=1)
        a = jnp.concatenate(
            [jnp.dot(o[h * t:(h + 1) * t].astype(BF16), wuv_ref[h], preferred_element_type=F32)
             for h in range(N_HEADS)], axis=1)
        merged = sga_ref[0].astype(F32) * a + gbb_ref[0].astype(F32)
        out_ref[0] = merged.astype(BF16)


def _attn(q, kcat, kmeta, sga, gbb, wts, t):
    B, L, _ = q.shape
    nq = L // t
    pairs = [(i, j) for i in range(nq) for j in range(i + 1)]
    qi = jnp.asarray([p[0] for p in pairs], jnp.int32)
    kj = jnp.asarray([p[1] for p in pairs], jnp.int32)
    rows = N_HEADS * t
    grid_spec = pltpu.PrefetchScalarGridSpec(
        num_scalar_prefetch=2,
        grid=(B, len(pairs)),
        in_specs=[
            pl.BlockSpec((1, t, Q_COLS), lambda b, s, qi, kj: (b, qi[s], 0)),
            pl.BlockSpec((1, t, KCAT_W), lambda b, s, qi, kj: (b, kj[s], 0)),
            pl.BlockSpec((LANES, KCAT_W), lambda b, s, qi, kj: (0, 0)),
            pl.BlockSpec((N_HEADS // 2, LANES, 2 * KV_RANK), lambda b, s, qi, kj: (0, 0, 0)),
            pl.BlockSpec((N_HEADS, KV_RANK, V_DIM), lambda b, s, qi, kj: (0, 0, 0)),
            pl.BlockSpec((1, t, D_MODEL), lambda b, s, qi, kj: (b, qi[s], 0)),
            pl.BlockSpec((1, t, D_MODEL), lambda b, s, qi, kj: (b, qi[s], 0)),
        ],
        out_specs=pl.BlockSpec((1, t, D_MODEL), lambda b, s, qi, kj: (b, qi[s], 0)),
        scratch_shapes=[
            pltpu.VMEM((rows, KCAT_W), BF16),
            pltpu.VMEM((rows, LANES), F32),
            pltpu.VMEM((rows, LANES), F32),
            pltpu.VMEM((rows, KV_RANK), F32),
        ],
    )
    return pl.pallas_call(
        functools.partial(_attn_kernel, t=t),
        out_shape=jax.ShapeDtypeStruct((B, L, D_MODEL), BF16),
        grid_spec=grid_spec,
        compiler_params=pltpu.CompilerParams(
            dimension_semantics=("arbitrary", "arbitrary"), vmem_limit_bytes=VMEM_LIMIT),
        name="attn",
    )(qi, kj, q, kcat, kmeta, wts["w_ukp"], wts["w_uv"], sga, gbb)


def _sfront_kernel(x_ref, g1_ref, w_ref, qg_ref, wuq_ref, kvg_ref, cos_ref, sin_ref, wpool_ref,
                   pscale_ref, hist_ref, wukp_ref,
                   qlat_ref, q1_ref, q2_ref, ckv_ref, kr_ref, p_ref, sga_ref, gbb_ref):
    n = x_ref.shape[0]
    x = x_ref[...]
    hb = _rms(x, g1_ref[...]).astype(BF16)
    z = jnp.dot(hb, w_ref[...], preferred_element_type=F32)
    cos = cos_ref[...]
    sin = sin_ref[...]

    qn = _rms(z[:, C_CQ:C_CQ + Q_RANK], qg_ref[...]).astype(BF16)
    q = jnp.dot(qn, wuq_ref[...], preferred_element_type=F32) * Q_SCALE
    q1 = q[:, Q_NOPE_W:Q_NOPE_W + LANES]
    q2 = q[:, Q_NOPE_W + LANES:]
    q1_ref[...] = q1 * cos - q2 * sin
    q2_ref[...] = q1 * sin + q2 * cos
    qb = q[:, 0:Q_NOPE_W].astype(BF16)
    for jp in range(N_HEADS // 2):
        ql = jnp.dot(qb[:, jp * LANES:(jp + 1) * LANES], wukp_ref[jp], preferred_element_type=F32)
        qlat_ref[2 * jp] = ql[:, 0:KV_RANK].astype(BF16)
        qlat_ref[2 * jp + 1] = ql[:, KV_RANK:].astype(BF16)

    ckv = _rms(z[:, C_CKV:C_CKV + KV_RANK], kvg_ref[...])
    ckv_ref[...] = ckv
    k1 = z[:, C_K1:C_K1 + LANES]
    k2 = z[:, C_K2:C_K2 + LANES]
    k1r = k1 * cos - k2 * sin
    k2r = k1 * sin + k2 * cos
    kr_ref[...] = _rope_pair_to_rows(k1r, k2r)

    p = z[:, C_P:C_P + POOL_WIDTH]
    p_ref[...] = p
    bs = []
    for g, w in enumerate(POOL_WINDOWS):
        lo = g * POOL_GROUP
        acc = p[:, lo:lo + POOL_GROUP]
        for k in range(1, w):
            acc = acc + hist_ref[POOL_HIST - k, :, lo:lo + POOL_GROUP]
        pooled = acc * (1.0 / w) - p[:, lo:lo + POOL_GROUP]
        bs.append(jnp.dot(pooled.astype(BF16), wpool_ref[g], preferred_element_type=F32))
    b = jnp.concatenate(bs, axis=1) * pscale_ref[...]
    sga_ref[...] = _sigmoid(z[:, C_GA:C_GA + D_MODEL]).astype(BF16)
    gbb_ref[...] = (_sigmoid(z[:, C_GB:C_GB + D_MODEL]) * b).astype(BF16)


def _sfront(xs, cos_s, sin_s, hist_t, wts):
    n = xs.shape[0]
    out_shape = (
        jax.ShapeDtypeStruct((N_HEADS, n, KV_RANK), BF16),
        jax.ShapeDtypeStruct((n, LANES), F32),
        jax.ShapeDtypeStruct((n, LANES), F32),
        jax.ShapeDtypeStruct((n, KV_RANK), F32),
        jax.ShapeDtypeStruct((n, ROPE_DIM), F32),
        jax.ShapeDtypeStruct((n, POOL_WIDTH), F32),
        jax.ShapeDtypeStruct((n, D_MODEL), BF16),
        jax.ShapeDtypeStruct((n, D_MODEL), BF16),
    )
    return pl.pallas_call(
        _sfront_kernel,
        out_shape=out_shape,
        compiler_params=pltpu.CompilerParams(vmem_limit_bytes=VMEM_LIMIT),
        name="sfront",
    )(xs, wts["g1"], wts["w_in"], wts["qg"], wts["w_uq"], wts["kvg"], cos_s, sin_s, wts["w_pool"],
      wts["pool_scale"], hist_t, wts["w_ukp"])


def _sattn_kernel(pt_ref, qlat_ref, qexp_ref, qrope_ref, ckvn_ref, krn_ref, cckv_hbm, ckrt_hbm,
                  o_ref, kbuf, rbuf, sem, m_ref, l_ref, acc_ref, *, pc, nc):
    b = pl.program_id(0)
    c = pl.program_id(1)
    nb = pl.num_programs(0)
    step = b * nc + c
    slot = step % 2
    nt_dims = (((1,), (1,)), ((), ()))

    def start_pages(bb, cc, sl):
        base = bb * (nc * pc) + cc * pc

        def body(pg, carry):
            phys = pt_ref[base + pg]
            pltpu.make_async_copy(cckv_hbm.at[phys], kbuf.at[sl, pg], sem.at[0, sl]).start()
            pltpu.make_async_copy(ckrt_hbm.at[phys], rbuf.at[sl, pg], sem.at[1, sl]).start()
            return carry

        lax.fori_loop(0, pc, body, 0)

    @pl.when(step == 0)
    def _():
        start_pages(b, c, slot)

    nxt = step + 1

    @pl.when(nxt < nb * nc)
    def _():
        start_pages(nxt // nc, nxt % nc, 1 - slot)

    ql = qlat_ref[0]

    @pl.when(c == 0)
    def _():
        cn = ckvn_ref[0]
        s0 = (jnp.sum(ql.astype(F32) * cn, axis=1, keepdims=True)
              + jnp.sum(qrope_ref[0] * krn_ref[0], axis=1, keepdims=True))
        m_ref[...] = jnp.broadcast_to(s0, (N_HEADS, LANES))
        l_ref[...] = jnp.ones((N_HEADS, LANES), F32)
        acc_ref[...] = jnp.broadcast_to(cn, (N_HEADS, KV_RANK))

    pltpu.make_async_copy(cckv_hbm.at[pl.ds(0, pc)], kbuf.at[slot], sem.at[0, slot]).wait()
    pltpu.make_async_copy(ckrt_hbm.at[pl.ds(0, pc)], rbuf.at[slot], sem.at[1, slot]).wait()

    kb = kbuf[slot].reshape(pc * PAGE_SIZE, KV_RANK).astype(BF16)
    s_lat = lax.dot_general(ql, kb, nt_dims, preferred_element_type=F32)
    qexp = qexp_ref[0]
    pieces = []
    for g in range(pc // ROPE_GROUP):
        r8 = rbuf[slot, g * ROPE_GROUP:(g + 1) * ROPE_GROUP].reshape(ROPE_GROUP * ROPE_DIM, PAGE_SIZE)
        og = jnp.dot(qexp, r8.astype(BF16), preferred_element_type=F32)
        pieces += [og[pp * N_HEADS:(pp + 1) * N_HEADS] for pp in range(ROPE_GROUP)]
    s = s_lat + jnp.concatenate(pieces, axis=1)
    m_prev = m_ref[...]
    m_new = jnp.maximum(m_prev, jnp.max(s, axis=1, keepdims=True))
    alpha = jnp.exp2(m_prev - m_new)
    p = jnp.exp2(s - m_new[:, 0:1])
    l_ref[...] = alpha * l_ref[...] + jnp.sum(p, axis=1, keepdims=True)
    m_ref[...] = m_new
    pv = jnp.dot(p.astype(BF16), kb, preferred_element_type=F32)
    acc_ref[...] = acc_ref[...] * jnp.concatenate([alpha] * (KV_RANK // LANES), axis=1) + pv

    @pl.when(c == nc - 1)
    def _():
        inv = 1.0 / l_ref[...]
        o_ref[0] = acc_ref[...] * jnp.concatenate([inv] * (KV_RANK // LANES), axis=1)


def _sattn(page_table, qlat, qexp, qrope, ckv_new, kr_new, cache_ckv, cache_krt, pc):
    n, n_pages = page_table.shape
    nc = n_pages // pc
    grid_spec = pltpu.PrefetchScalarGridSpec(
        num_scalar_prefetch=1,
        grid=(n, nc),
        in_specs=[
            pl.BlockSpec((1, N_HEADS, KV_RANK), lambda b, c, pt: (b, 0, 0)),
            pl.BlockSpec((1, ROPE_GROUP * N_HEADS, ROPE_GROUP * ROPE_DIM), lambda b, c, pt: (b, 0, 0)),
            pl.BlockSpec((1, N_HEADS, ROPE_DIM), lambda b, c, pt: (b, 0, 0)),
            pl.BlockSpec((1, 1, KV_RANK), lambda b, c, pt: (b, 0, 0)),
            pl.BlockSpec((1, 1, ROPE_DIM), lambda b, c, pt: (b, 0, 0)),
            pl.BlockSpec(memory_space=pl.ANY),
            pl.BlockSpec(memory_space=pl.ANY),
        ],
        out_specs=pl.BlockSpec((1, N_HEADS, KV_RANK), lambda b, c, pt: (b, 0, 0)),
        scratch_shapes=[
            pltpu.VMEM((2, pc, PAGE_SIZE, KV_RANK), F32),
            pltpu.VMEM((2, pc, ROPE_DIM, PAGE_SIZE), F32),
            pltpu.SemaphoreType.DMA((2, 2)),
            pltpu.VMEM((N_HEADS, LANES), F32),
            pltpu.VMEM((N_HEADS, LANES), F32),
            pltpu.VMEM((N_HEADS, KV_RANK), F32),
        ],
    )
    return pl.pallas_call(
        functools.partial(_sattn_kernel, pc=pc, nc=nc),
        out_shape=jax.ShapeDtypeStruct((n, N_HEADS, KV_RANK), F32),
        grid_spec=grid_spec,
        compiler_params=pltpu.CompilerParams(
            dimension_semantics=("arbitrary", "arbitrary"), vmem_limit_bytes=VMEM_LIMIT),
        name="sattn",
    )(page_table.reshape(-1), qlat, qexp, qrope, ckv_new, kr_new, cache_ckv, cache_krt)


def _smerge_kernel(o_ref, wuv_ref, sga_ref, gbb_ref, out_ref):
    a = jnp.concatenate(
        [jnp.dot(o_ref[h].astype(BF16), wuv_ref[h], preferred_element_type=F32)
         for h in range(N_HEADS)], axis=1)
    out_ref[...] = (sga_ref[...].astype(F32) * a + gbb_ref[...].astype(F32)).astype(BF16)


def _smerge(o_t, sga, gbb, wts):
    n = sga.shape[0]
    return pl.pallas_call(
        _smerge_kernel,
        out_shape=jax.ShapeDtypeStruct((n, D_MODEL), BF16),
        name="smerge",
    )(o_t, wts["w_uv"], sga, gbb)


def _post_kernel(x_ref, mg_ref, wout_ref, g2_ref, wr_ref, br_ref, x1_ref, h2_ref, ti_ref, tg_ref):
    tm = x_ref.shape[0]
    x1 = x_ref[...] + jnp.dot(mg_ref[...], wout_ref[...], preferred_element_type=F32)
    x1_ref[...] = x1
    h2 = _rms(x1, g2_ref[...])
    _store_row_tiles(h2_ref, h2)
    hh = h2.astype(BF16)
    hl = (h2 - hh.astype(F32)).astype(BF16)
    o1 = jnp.dot(hh, wr_ref[...], preferred_element_type=F32)
    o2 = jnp.dot(hl, wr_ref[:, 0:LANES], preferred_element_type=F32)
    logits = o1[:, 0:LANES] + o1[:, LANES:] + o2 + br_ref[...]
    lane = lax.broadcasted_iota(jnp.int32, (tm, LANES), 1)
    vals = logits
    tops, idxs = [], []
    for _ in range(TOP_K):
        m = jnp.max(vals, axis=1, keepdims=True)
        idx = jnp.min(jnp.where(vals == m, lane, LANES), axis=1, keepdims=True)
        tops.append(m)
        idxs.append(idx)
        vals = jnp.where(lane == idx, NEG_BIG * 2, vals)
    es = [jnp.exp(v - tops[0]) for v in tops]
    den = es[0] + es[1] + es[2] + es[3]
    ti = jnp.zeros((tm, LANES), jnp.int32)
    tg = jnp.zeros((tm, LANES), F32)
    for k in range(TOP_K):
        ti = jnp.where(lane == k, idxs[k], ti)
        tg = jnp.where(lane == k, es[k] / den, tg)
    ti_ref[...] = ti
    tg_ref[...] = tg


def _post(x, merged, wts, tm):
    T = x.shape[0]
    full = lambda shape: pl.BlockSpec(shape, lambda i: (0,) * len(shape))
    row = lambda w: pl.BlockSpec((tm, w), lambda i: (i, 0))
    return pl.pallas_call(
        _post_kernel,
        out_shape=(
            jax.ShapeDtypeStruct((T, D_MODEL), F32),
            jax.ShapeDtypeStruct((T * ROW_SUB, LANES), F32),
            jax.ShapeDtypeStruct((T, LANES), jnp.int32),
            jax.ShapeDtypeStruct((T, LANES), F32),
        ),
        grid=(T // tm,),
        in_specs=[row(D_MODEL), row(D_MODEL), full((D_MODEL, D_MODEL)), full((1, D_MODEL)),
                  full((D_MODEL, 2 * LANES)), full((1, LANES))],
        out_specs=(row(D_MODEL), pl.BlockSpec((tm * ROW_SUB, LANES), lambda i: (i, 0)),
                   row(LANES), row(LANES)),
        compiler_params=pltpu.CompilerParams(
            dimension_semantics=("arbitrary",), vmem_limit_bytes=VMEM_LIMIT),
        name="post",
    )(x, merged, wts["w_out"], wts["g2"], wts["w_r"], wts["b_r"])


def _dispatch_kernel(plan_ref, dest_ref, hp_ref, hs_ref, xs_hbm, hbuf, sem, psem, *, td, ntp, n_slots):
    i = pl.program_id(0)
    n = pl.num_programs(0)
    slot = i % 2

    def wait_tile(s):
        span = pl.ds(0, td * TOP_K * ROW_SUB)
        pltpu.make_async_copy(xs_hbm.at[span], xs_hbm.at[span], sem.at[s]).wait()

    @pl.when(i >= 2)
    def _():
        wait_tile(slot)

    @pl.when(i < ntp)
    def _():
        hbuf[slot] = hp_ref[...]

    @pl.when(i >= ntp)
    def _():
        hbuf[slot] = hs_ref[...]

    def body(r8, c):
        for rr in range(8):
            r = r8 * 8 + rr
            for k in range(TOP_K):
                d = dest_ref[0, 0, r * TOP_K + k]
                pltpu.make_async_copy(_row_tile(hbuf.at[slot], r), _row_tile(xs_hbm, d),
                                      sem.at[slot]).start(priority=k % 2)
        return c

    lax.fori_loop(0, td // 8, body, 0)

    @pl.when(i == n - 1)
    def _():
        def fill(s, c):
            pltpu.make_async_copy(_row_tile(hbuf.at[slot], 0), _row_tile(xs_hbm, s), psem.at[0]).start()
            return c

        def drain(s, c):
            pltpu.make_async_copy(_row_tile(hbuf.at[slot], 0), _row_tile(xs_hbm, s), psem.at[0]).wait()
            return c

        def per_expert(e, c):
            lo = plan_ref[e]
            hi = plan_ref[N_EXPERTS + e]
            lax.fori_loop(lo, hi, fill, 0)
            lax.fori_loop(lo, hi, drain, 0)
            return c

        lax.fori_loop(0, N_EXPERTS, per_expert, 0)

        def fill_tail(c, cc):
            rows = pl.ds(pl.multiple_of(c * (td * ROW_SUB), td * ROW_SUB), td * ROW_SUB)
            cp = pltpu.make_async_copy(hbuf.at[slot], xs_hbm.at[rows], psem.at[0])
            cp.start()
            cp.wait()
            return cc

        lax.fori_loop(plan_ref[2 * N_EXPERTS - 1] // td, n_slots // td, fill_tail, 0)
        wait_tile(slot)

        @pl.when(n >= 2)
        def _():
            wait_tile(1 - slot)


def _dispatch(plan, dest, h2_p, h2_s, cap, td):
    T = dest.shape[0]
    nt = T // td
    ntp = h2_p.shape[0] // (td * ROW_SUB)
    tile_spec = lambda f: pl.BlockSpec((td * ROW_SUB, LANES), f)
    grid_spec = pltpu.PrefetchScalarGridSpec(
        num_scalar_prefetch=1,
        grid=(nt,),
        in_specs=[
            pl.BlockSpec((1, 1, td * TOP_K), lambda i, plan: (i, 0, 0), memory_space=pltpu.SMEM),
            tile_spec(lambda i, plan: (jnp.minimum(i, ntp - 1), 0)),
            tile_spec(lambda i, plan: (jnp.maximum(i - ntp, 0), 0)),
        ],
        out_specs=pl.BlockSpec(memory_space=pl.ANY),
        scratch_shapes=[pltpu.VMEM((2, td * ROW_SUB, LANES), F32), pltpu.SemaphoreType.DMA((2,)),
                        pltpu.SemaphoreType.DMA((1,))],
    )
    return pl.pallas_call(
        functools.partial(_dispatch_kernel, td=td, ntp=ntp, n_slots=cap),
        out_shape=jax.ShapeDtypeStruct((cap * ROW_SUB, LANES), F32),
        grid_spec=grid_spec,
        compiler_params=pltpu.CompilerParams(dimension_semantics=("arbitrary",)),
        name="dispatch",
    )(plan, dest.reshape(nt, 1, td * TOP_K), h2_p, h2_s)


def _experts_kernel(te_ref, tf_ref, tv_ref, ts_ref, xs_ref, wgu_ref, bgu_ref, wdn_ref, bdn_ref, out_ref,
                    wgu_bf, wdn_bf):
    t = pl.program_id(0)

    @pl.when(tf_ref[t] == 1)
    def _():
        wgu_bf[...] = wgu_ref[0].astype(BF16)
        wdn_bf[...] = wdn_ref[0].astype(BF16)

    @pl.when(tv_ref[t] == 1)
    def _():
        x = _load_row_tiles(xs_ref, xs_ref.shape[0] // ROW_SUB).astype(BF16)
        gu = jnp.dot(x, wgu_bf[...], preferred_element_type=F32) + bgu_ref[0]
        gate = jnp.minimum(gu[:, 0:D_FF], SWIGLU_LIMIT)
        up = jnp.clip(gu[:, D_FF:], -SWIGLU_LIMIT, SWIGLU_LIMIT)
        act = gate * _sigmoid(gate * SWIGLU_ALPHA)
        hmid = ((up + 1.0) * act).astype(BF16)
        _store_row_tiles(out_ref, jnp.dot(hmid, wdn_bf[...], preferred_element_type=F32) + bdn_ref[0])

    @pl.when(tv_ref[t] == 0)
    def _():
        out_ref[...] = jnp.zeros_like(out_ref)


def _experts(tile_e, tile_first, tile_valid, tile_src, xs, w_gate_up, b_gate_up, w_down, b_down, tm):
    cap = xs.shape[0] // ROW_SUB
    n_tiles = cap // tm
    grid_spec = pltpu.PrefetchScalarGridSpec(
        num_scalar_prefetch=4,
        grid=(n_tiles,),
        in_specs=[
            pl.BlockSpec((tm * ROW_SUB, LANES), lambda t, te, tf, tv, ts: (ts[t], 0)),
            pl.BlockSpec((1, D_MODEL, 2 * D_FF), lambda t, te, tf, tv, ts: (te[t], 0, 0)),
            pl.BlockSpec((1, 1, 2 * D_FF), lambda t, te, tf, tv, ts: (te[t], 0, 0)),
            pl.BlockSpec((1, D_FF, D_MODEL), lambda t, te, tf, tv, ts: (te[t], 0, 0)),
            pl.BlockSpec((1, 1, D_MODEL), lambda t, te, tf, tv, ts: (te[t], 0, 0)),
        ],
        out_specs=pl.BlockSpec((tm * ROW_SUB, LANES), lambda t, te, tf, tv, ts: (t, 0)),
        scratch_shapes=[pltpu.VMEM((D_MODEL, 2 * D_FF), BF16), pltpu.VMEM((D_FF, D_MODEL), BF16)],
    )
    return pl.pallas_call(
        _experts_kernel,
        out_shape=jax.ShapeDtypeStruct((cap * ROW_SUB, LANES), F32),
        grid_spec=grid_spec,
        compiler_params=pltpu.CompilerParams(
            dimension_semantics=("arbitrary",), vmem_limit_bytes=VMEM_LIMIT),
        name="experts",
    )(tile_e, tile_first, tile_valid, tile_src, xs, w_gate_up, b_gate_up.reshape(N_EXPERTS, 1, 2 * D_FF),
      w_down, b_down.reshape(N_EXPERTS, 1, D_MODEL))


def _combine_kernel(dcur_ref, dnxt_ref, x1_ref, tg_ref, gf_ref, yp_hbm, out_ref, buf, sem, *, tm):
    i = pl.program_id(0)
    n = pl.num_programs(0)
    slot = i % 2

    def start_rows(dref, sl):
        def body(r8, c):
            for rr in range(8):
                r = r8 * 8 + rr
                for k in range(TOP_K):
                    d = dref[0, 0, r * TOP_K + k]
                    pltpu.make_async_copy(_row_tile(yp_hbm, d), _row_tile(buf.at[sl], k * tm + r),
                                          sem.at[sl]).start(priority=k % 2)
            return c

        lax.fori_loop(0, tm // 8, body, 0)

    @pl.when(i == 0)
    def _():
        start_rows(dcur_ref, slot)

    @pl.when(i + 1 < n)
    def _():
        start_rows(dnxt_ref, 1 - slot)

    pltpu.make_async_copy(yp_hbm.at[pl.ds(0, TOP_K * tm * ROW_SUB)], buf.at[slot], sem.at[slot]).wait()
    y = x1_ref[...]
    tg = tg_ref[...]
    for k in range(TOP_K):
        y = y + _load_row_tiles(buf.at[slot], tm, first=k * tm) * tg[:, k:k + 1]
    out_ref[...] = _rms(y, gf_ref[...])


def _combine(dest, x1, tg, gf, yp, tm):
    T = x1.shape[0]
    nt = T // tm
    dest3 = dest.reshape(nt, 1, tm * TOP_K)
    return pl.pallas_call(
        functools.partial(_combine_kernel, tm=tm),
        out_shape=jax.ShapeDtypeStruct((T, D_MODEL), F32),
        grid=(nt,),
        in_specs=[
            pl.BlockSpec((1, 1, tm * TOP_K), lambda i: (i, 0, 0), memory_space=pltpu.SMEM),
            pl.BlockSpec((1, 1, tm * TOP_K), lambda i: (jnp.minimum(i + 1, nt - 1), 0, 0),
                         memory_space=pltpu.SMEM),
            pl.BlockSpec((tm, D_MODEL), lambda i: (i, 0)),
            pl.BlockSpec((tm, LANES), lambda i: (i, 0)),
            pl.BlockSpec((1, D_MODEL), lambda i: (0, 0)),
            pl.BlockSpec(memory_space=pl.ANY),
        ],
        out_specs=pl.BlockSpec((tm, D_MODEL), lambda i: (i, 0)),
        scratch_shapes=[pltpu.VMEM((2, TOP_K * tm * ROW_SUB, LANES), F32), pltpu.SemaphoreType.DMA((2,))],
        compiler_params=pltpu.CompilerParams(
            dimension_semantics=("arbitrary",), vmem_limit_bytes=VMEM_LIMIT),
        name="combine",
    )(dest3, dest3, x1, tg, gf, yp)


def _rope_tables(pos):
    inv = ROPE_BASE ** (-jnp.arange(ROPE_HALF, dtype=F32) / ROPE_HALF)
    ang = pos.astype(F32)[:, None] * inv[None, :]
    return jnp.tile(jnp.cos(ang), (1, N_HEADS)), jnp.tile(jnp.sin(ang), (1, N_HEADS))


def _prep_weights(norm1_g, w_in, q_norm_g, w_uq, kv_norm_g, w_uk, w_uv, w_pool, pool_scale, w_out,
                  norm2_g, w_router, b_router):
    i0 = Q_RANK
    i1 = i0 + KV_RANK
    i2 = i1 + ROPE_DIM
    i3 = i2 + POOL_WIDTH
    w_packed = jnp.concatenate([
        w_in[:, 0:i1],
        jnp.tile(w_in[:, i1:i1 + ROPE_HALF], (1, N_HEADS)),
        jnp.tile(w_in[:, i1 + ROPE_HALF:i2], (1, N_HEADS)),
        w_in[:, i2:],
    ], axis=1).astype(BF16)
    wq = w_uq.reshape(Q_RANK, N_HEADS, QK_DIM)
    wq_packed = jnp.concatenate([
        wq[:, :, 0:NOPE_DIM].reshape(Q_RANK, Q_NOPE_W),
        wq[:, :, NOPE_DIM:NOPE_DIM + ROPE_HALF].reshape(Q_RANK, LANES),
        wq[:, :, NOPE_DIM + ROPE_HALF:].reshape(Q_RANK, LANES),
    ], axis=1).astype(BF16)
    zeros = jnp.zeros((NOPE_DIM, KV_RANK), F32)
    w_ukp = jnp.stack([
        jnp.concatenate([
            jnp.concatenate([w_uk[2 * jp], zeros], axis=1),
            jnp.concatenate([zeros, w_uk[2 * jp + 1]], axis=1)], axis=0)
        for jp in range(N_HEADS // 2)]).astype(BF16)
    wr_hi = w_router.astype(BF16)
    wr_lo = (w_router - wr_hi.astype(F32)).astype(BF16)
    pad = ((0, 0), (0, LANES - N_EXPERTS))
    w_r = jnp.concatenate([jnp.pad(wr_hi, pad), jnp.pad(wr_lo, pad)], axis=1)
    b_r = jnp.concatenate([b_router.astype(F32), jnp.full((LANES - N_EXPERTS,), NEG_BIG, F32)])[None, :]
    return {
        "g1": norm1_g[None, :], "w_in": w_packed, "qg": q_norm_g[None, :], "w_uq": wq_packed,
        "kvg": kv_norm_g[None, :], "w_ukp": w_ukp, "w_uv": w_uv.astype(BF16),
        "w_pool": w_pool.astype(BF16), "pool_scale": pool_scale[None, :], "w_out": w_out.astype(BF16),
        "g2": norm2_g[None, :], "w_r": w_r, "b_r": b_r,
    }


def _route(top_i, tm):
    T = top_i.shape[0]
    n_asg = T * TOP_K
    n_tiles = -(-n_asg // tm) + N_EXPERTS
    e = top_i.reshape(-1)
    onehot = (e[:, None] == jnp.arange(N_EXPERTS, dtype=jnp.int32)[None, :]).astype(jnp.int32)
    csum = jnp.cumsum(onehot, axis=0)
    counts = csum[-1]
    padded = (counts + tm - 1) // tm * tm
    pend = jnp.cumsum(padded)
    pstart = pend - padded
    dest = jnp.sum(onehot * (csum - 1 + pstart[None, :]), axis=1)
    tile_start = jnp.arange(n_tiles, dtype=jnp.int32) * tm
    tile_e = jnp.minimum(jnp.sum((pend[None, :] <= tile_start[:, None]).astype(jnp.int32), axis=1), N_EXPERTS - 1)
    n_used = pend[-1] // tm
    tile_valid = (tile_start < pend[-1]).astype(jnp.int32)
    tile_first = jnp.concatenate([jnp.ones((1,), jnp.int32), (tile_e[1:] != tile_e[:-1]).astype(jnp.int32)])
    tile_src = jnp.minimum(jnp.arange(n_tiles, dtype=jnp.int32), n_used - 1)
    plan = jnp.concatenate([pstart + counts, pend]).astype(jnp.int32)
    return dest.reshape(T, TOP_K).astype(jnp.int32), plan, tile_e.astype(jnp.int32), tile_first, tile_valid, tile_src


def kernel(x_prompt, x_sample, cache_ckv, cache_krope, state_pool, page_table, meta_tokens, norm1_g, w_in, q_norm_g, w_uq, kv_norm_g, w_uk, w_uv, w_pool, pool_scale, w_out, norm2_g, w_router, b_router, w_gate_up, b_gate_up, w_down, b_down, final_norm_g):
    B, L, _ = x_prompt.shape
    n = x_sample.shape[0]
    n_pages = page_table.shape[1]
    past_len = n_pages * PAGE_SIZE
    wts = _prep_weights(norm1_g[0], w_in[0], q_norm_g[0], w_uq[0], kv_norm_g[0], w_uk[0], w_uv[0],
                        w_pool[0], pool_scale[0], w_out[0], norm2_g[0], w_router[0], b_router[0])

    cos_m, sin_m = _rope_tables(jnp.arange(N_META))
    _, kcat_m, ckv_m, kr_m, p_m, _, _ = _front(
        meta_tokens[None], cos_m, sin_m, jnp.zeros((16, POOL_WIDTH), F32), wts, N_META)

    cos_p, sin_p = _rope_tables(N_META + jnp.arange(L))
    q, kcat, ckv_p, kr_p, ptail, sga, gbb = _front(x_prompt, cos_p, sin_p, p_m[0], wts, FRONT_TM)
    kmeta = jnp.pad(kcat_m[0], ((0, LANES - N_META), (0, 0)))
    merged_p = _attn(q, kcat, kmeta, sga, gbb, wts, ATTN_T)

    cos_s, sin_s = _rope_tables(jnp.full((n,), past_len))
    hist_t = jnp.transpose(state_pool[0], (1, 0, 2))
    xs = x_sample[:, 0, :]
    qlat_t, q1_s, q2_s, ckv_s, kr_s, p_s, sga_s, gbb_s = _sfront(xs, cos_s, sin_s, hist_t, wts)
    qlat_s = jnp.transpose(qlat_t, (1, 0, 2))
    qrope_s = jnp.concatenate([q1_s.reshape(n, N_HEADS, ROPE_HALF), q2_s.reshape(n, N_HEADS, ROPE_HALF)], axis=2)
    eye = jnp.eye(ROPE_GROUP, dtype=F32)
    qexp_s = (eye[None, :, None, :, None] * qrope_s[:, None, :, None, :]).reshape(
        n, ROPE_GROUP * N_HEADS, ROPE_GROUP * ROPE_DIM).astype(BF16)
    o_s = _sattn(page_table, qlat_s, qexp_s, qrope_s, ckv_s[:, None, :], kr_s[:, None, :],
                 cache_ckv[0], jnp.swapaxes(cache_krope[0], 1, 2), min(SATTN_PAGES, n_pages))
    merged_s = _smerge(jnp.transpose(o_s, (1, 0, 2)), sga_s, gbb_s, wts)

    Tp = B * L
    x1_p, h2_p, ti_p, tg_p = _post(x_prompt.reshape(Tp, D_MODEL), merged_p.reshape(Tp, D_MODEL), wts, POST_TM)
    x1_s, h2_s, ti_s, tg_s = _post(xs, merged_s, wts, n)
    top_i = jnp.concatenate([ti_p[:, 0:TOP_K], ti_s[:, 0:TOP_K]], axis=0)
    dest, plan, tile_e, tile_first, tile_valid, tile_src = _route(top_i, MOE_TM)
    cap = tile_e.shape[0] * MOE_TM
    xs_g = _dispatch(plan, dest, h2_p, h2_s, cap, DISPATCH_TD)
    yp = _experts(tile_e, tile_first, tile_valid, tile_src, xs_g, w_gate_up[0], b_gate_up[0], w_down[0],
                  b_down[0], MOE_TM)
    gf = final_norm_g[None, :]
    y_prompt = _combine(dest[:Tp], x1_p, tg_p, gf, yp, POST_TM).reshape(B, L, D_MODEL)
    y_sample = _combine(dest[Tp:], x1_s, tg_s, gf, yp, n).reshape(n, 1, D_MODEL)
    new_ckv_prompt = jnp.concatenate([jnp.broadcast_to(ckv_m, (B, N_META, KV_RANK)), ckv_p], axis=1)[None]
    new_krope_prompt = jnp.concatenate([jnp.broadcast_to(kr_m, (B, N_META, ROPE_DIM)), kr_p], axis=1)[None]
    new_pool_prompt = ptail[:, 16 - POOL_HIST:, :][None]
    new_ckv_sample = ckv_s[None, :, None, :]
    new_krope_sample = kr_s[None, :, None, :]
    new_pool_sample = jnp.concatenate([state_pool[0][:, 1:, :], p_s[:, None, :]], axis=1)[None]
    return (y_prompt, y_sample, new_ckv_prompt, new_krope_prompt, new_pool_prompt,
            new_ckv_sample, new_krope_sample, new_pool_sample)
```

```python
import functools
import math

import jax
import jax.numpy as jnp
from jax import lax
from jax.experimental import pallas as pl
from jax.experimental.pallas import tpu as pltpu

F32 = jnp.float32
BF16 = jnp.bfloat16

D_MODEL = 1024
N_META = 16
N_HEADS = 8
Q_RANK = 256
KV_RANK = 256
NOPE_DIM = 64
ROPE_DIM = 32
ROPE_HALF = ROPE_DIM // 2
QK_DIM = NOPE_DIM + ROPE_DIM
V_DIM = D_MODEL // N_HEADS
ROPE_BASE = 10000.0
ATTN_SCALE = 1.0 / math.sqrt(QK_DIM)
Q_SCALE = ATTN_SCALE * math.log2(math.e)
POOL_WIDTH = D_MODEL // 2
POOL_WINDOWS = (2, 4, 8, 16)
POOL_GROUP = POOL_WIDTH // len(POOL_WINDOWS)
POOL_OUT_GROUP = D_MODEL // len(POOL_WINDOWS)
POOL_HIST = max(POOL_WINDOWS) - 1
N_EXPERTS = 32
TOP_K = 4
D_FF = D_MODEL
SWIGLU_LIMIT = 7.0
SWIGLU_ALPHA = 1.702
EPS = 1e-6
PAGE_SIZE = 128

LANES = 128
ROW_SUB = D_MODEL // LANES
NEG_BIG = -1e30

C_CQ = 0
C_CKV = C_CQ + Q_RANK
C_K1 = C_CKV + KV_RANK
C_K2 = C_K1 + LANES
C_P = C_K2 + LANES
C_GA = C_P + POOL_WIDTH
C_GB = C_GA + D_MODEL
C_END = C_GB + D_MODEL
Q_NOPE_W = N_HEADS * NOPE_DIM
Q_COLS = Q_NOPE_W + 2 * LANES
KCAT_W = KV_RANK + 2 * LANES

FRONT_TM = 512
ATTN_T = 256
POST_TM = 512
MOE_TM = 512
SATTN_PAGES = 128
ROPE_GROUP = 2 * LANES // ROPE_DIM
DISPATCH_TD = 128
VMEM_LIMIT = 56 * 1024 * 1024


def _rms(x, g):
    return x * lax.rsqrt(jnp.mean(x * x, axis=-1, keepdims=True) + EPS) * g


def _sigmoid(x):
    return 1.0 / (1.0 + jnp.exp(-x))


def _store_row_tiles(ref, x):
    n = x.shape[0]
    for s in range(ROW_SUB):
        ref[pl.ds(s, n, stride=ROW_SUB), :] = x[:, s * LANES:(s + 1) * LANES]


def _load_row_tiles(ref, n, first=0):
    return jnp.concatenate(
        [ref[pl.ds(first * ROW_SUB + s, n, stride=ROW_SUB), :] for s in range(ROW_SUB)], axis=1)


def _row_tile(ref, r):
    return ref.at[pl.ds(pl.multiple_of(r * ROW_SUB, ROW_SUB), ROW_SUB)]


def _rope_pair_to_rows(k1r, k2r):
    lane = lax.broadcasted_iota(jnp.int32, k1r.shape, 1)
    return jnp.where(lane < ROPE_HALF, k1r, pltpu.roll(k2r, ROPE_HALF, axis=1))[:, 0:ROPE_DIM]


def _front_kernel(x_ref, g1_ref, w_ref, qg_ref, wuq_ref, kvg_ref, cos_ref, sin_ref, wpool_ref,
                  pscale_ref, pprev_ref,
                  q_ref, kcat_ref, ckv_ref, kr_ref, ptail_ref, sga_ref, gbb_ref,
                  pext_ref, *, tm):
    i = pl.program_id(1)
    x = x_ref[0]
    hb = _rms(x, g1_ref[...]).astype(BF16)
    z = jnp.dot(hb, w_ref[...], preferred_element_type=F32)
    cos = cos_ref[...]
    sin = sin_ref[...]

    qn = _rms(z[:, C_CQ:C_CQ + Q_RANK], qg_ref[...]).astype(BF16)
    q = jnp.dot(qn, wuq_ref[...], preferred_element_type=F32) * Q_SCALE
    q1 = q[:, Q_NOPE_W:Q_NOPE_W + LANES]
    q2 = q[:, Q_NOPE_W + LANES:]
    q_ref[0, :, 0:Q_NOPE_W] = q[:, 0:Q_NOPE_W].astype(BF16)
    q_ref[0, :, Q_NOPE_W:Q_NOPE_W + LANES] = (q1 * cos - q2 * sin).astype(BF16)
    q_ref[0, :, Q_NOPE_W + LANES:] = (q1 * sin + q2 * cos).astype(BF16)

    ckv = _rms(z[:, C_CKV:C_CKV + KV_RANK], kvg_ref[...])
    ckv_ref[0] = ckv
    k1 = z[:, C_K1:C_K1 + LANES]
    k2 = z[:, C_K2:C_K2 + LANES]
    k1r = k1 * cos - k2 * sin
    k2r = k1 * sin + k2 * cos
    kcat_ref[0, :, 0:KV_RANK] = ckv.astype(BF16)
    kcat_ref[0, :, KV_RANK:KV_RANK + LANES] = k1r.astype(BF16)
    kcat_ref[0, :, KV_RANK + LANES:] = k2r.astype(BF16)
    kr_ref[0] = _rope_pair_to_rows(k1r, k2r)

    p = z[:, C_P:C_P + POOL_WIDTH]

    @pl.when(i == 0)
    def _():
        pext_ref[0:16, :] = pprev_ref[...]

    @pl.when(i > 0)
    def _():
        pext_ref[0:16, :] = pext_ref[tm:tm + 16, :]

    pext_ref[16:16 + tm, :] = p
    ptail_ref[0] = p[tm - 16:tm, :]
    bs = []
    for g, w in enumerate(POOL_WINDOWS):
        lo = g * POOL_GROUP
        acc = pext_ref[16:16 + tm, lo:lo + POOL_GROUP]
        for k in range(1, w):
            acc = acc + pext_ref[16 - k:16 - k + tm, lo:lo + POOL_GROUP]
        pooled = acc * (1.0 / w) - p[:, lo:lo + POOL_GROUP]
        bs.append(jnp.dot(pooled.astype(BF16), wpool_ref[g], preferred_element_type=F32))
    b = jnp.concatenate(bs, axis=1) * pscale_ref[...]
    sga_ref[0] = _sigmoid(z[:, C_GA:C_GA + D_MODEL]).astype(BF16)
    gbb_ref[0] = (_sigmoid(z[:, C_GB:C_GB + D_MODEL]) * b).astype(BF16)


def _front(x, cos_t, sin_t, pprev, wts, tm):
    B, L, _ = x.shape
    nt = L // tm
    full = lambda shape: pl.BlockSpec(shape, lambda b, i: (0,) * len(shape))
    row = lambda w: pl.BlockSpec((1, tm, w), lambda b, i: (b, i, 0))
    out_shape = (
        jax.ShapeDtypeStruct((B, L, Q_COLS), BF16),
        jax.ShapeDtypeStruct((B, L, KCAT_W), BF16),
        jax.ShapeDtypeStruct((B, L, KV_RANK), F32),
        jax.ShapeDtypeStruct((B, L, ROPE_DIM), F32),
        jax.ShapeDtypeStruct((B, 16, POOL_WIDTH), F32),
        jax.ShapeDtypeStruct((B, L, D_MODEL), BF16),
        jax.ShapeDtypeStruct((B, L, D_MODEL), BF16),
    )
    return pl.pallas_call(
        functools.partial(_front_kernel, tm=tm),
        out_shape=out_shape,
        grid=(B, nt),
        in_specs=[
            row(D_MODEL),
            full((1, D_MODEL)),
            full((D_MODEL, C_END)),
            full((1, Q_RANK)),
            full((Q_RANK, Q_COLS)),
            full((1, KV_RANK)),
            pl.BlockSpec((tm, LANES), lambda b, i: (i, 0)),
            pl.BlockSpec((tm, LANES), lambda b, i: (i, 0)),
            full((len(POOL_WINDOWS), POOL_GROUP, POOL_OUT_GROUP)),
            full((1, D_MODEL)),
            full((16, POOL_WIDTH)),
        ],
        out_specs=(
            row(Q_COLS), row(KCAT_W), row(KV_RANK), row(ROPE_DIM),
            pl.BlockSpec((1, 16, POOL_WIDTH), lambda b, i: (b, 0, 0)),
            row(D_MODEL), row(D_MODEL),
        ),
        scratch_shapes=[pltpu.VMEM((tm + 16, POOL_WIDTH), F32)],
        compiler_params=pltpu.CompilerParams(
            dimension_semantics=("arbitrary", "arbitrary"), vmem_limit_bytes=VMEM_LIMIT),
        name="front",
    )(x, wts["g1"], wts["w_in"], wts["qg"], wts["w_uq"], wts["kvg"], cos_t, sin_t, wts["w_pool"],
      wts["pool_scale"], pprev)


def _build_qcat(q, wukp_ref, qcat_ref, t):
    lane = lax.broadcasted_iota(jnp.int32, (t, LANES), 1)
    q1 = q[:, Q_NOPE_W:Q_NOPE_W + LANES].astype(F32)
    q2 = q[:, Q_NOPE_W + LANES:].astype(F32)
    zero = jnp.zeros_like(q1)
    for jp in range(N_HEADS // 2):
        ql = jnp.dot(q[:, jp * LANES:(jp + 1) * LANES], wukp_ref[jp], preferred_element_type=F32)
        for s in range(2):
            h = 2 * jp + s
            sel = (lane >= h * ROPE_HALF) & (lane < (h + 1) * ROPE_HALF)
            qcat_ref[h * t:(h + 1) * t, 0:KV_RANK] = ql[:, s * KV_RANK:(s + 1) * KV_RANK].astype(BF16)
            qcat_ref[h * t:(h + 1) * t, KV_RANK:KV_RANK + LANES] = jnp.where(sel, q1, zero).astype(BF16)
            qcat_ref[h * t:(h + 1) * t, KV_RANK + LANES:] = jnp.where(sel, q2, zero).astype(BF16)


def _attn_kernel(qi_ref, kj_ref, q_ref, kcat_ref, kmeta_ref, wukp_ref, wuv_ref, sga_ref, gbb_ref,
                 out_ref, qcat_ref, m_ref, l_ref, acc_ref, *, t):
    s_id = pl.program_id(1)
    i = qi_ref[s_id]
    j = kj_ref[s_id]
    rows = N_HEADS * t
    nt_dims = (((1,), (1,)), ((), ()))

    @pl.when(j == 0)
    def _():
        _build_qcat(q_ref[0], wukp_ref, qcat_ref, t)
        km = kmeta_ref[...]
        s = lax.dot_general(qcat_ref[...], km, nt_dims, preferred_element_type=F32)
        col = lax.broadcasted_iota(jnp.int32, (rows, LANES), 1)
        s = jnp.where(col < N_META, s, NEG_BIG)
        m = jnp.max(s, axis=1, keepdims=True)
        p = jnp.exp2(s - m)
        m_ref[...] = jnp.broadcast_to(m, (rows, LANES))
        l_ref[...] = jnp.broadcast_to(jnp.sum(p, axis=1, keepdims=True), (rows, LANES))
        acc_ref[...] = jnp.dot(p.astype(BF16), km[:, 0:KV_RANK], preferred_element_type=F32)

    def kv_tile_update(on_diagonal):
        k = kcat_ref[0]
        s = lax.dot_general(qcat_ref[...], k, nt_dims, preferred_element_type=F32)
        if on_diagonal:
            r_tok = lax.broadcasted_iota(jnp.int32, (rows, t), 0) & (t - 1)
            col = lax.broadcasted_iota(jnp.int32, (rows, t), 1)
            s = jnp.where(col <= r_tok, s, NEG_BIG)
        m_prev = m_ref[...]
        m_new = jnp.maximum(m_prev, jnp.max(s, axis=1, keepdims=True))
        alpha = jnp.exp2(m_prev - m_new)
        p = jnp.exp2(s - jnp.concatenate([m_new] * (t // LANES), axis=1))
        l_ref[...] = alpha * l_ref[...] + jnp.sum(p, axis=1, keepdims=True)
        m_ref[...] = m_new
        pv = jnp.dot(p.astype(BF16), k[:, 0:KV_RANK], preferred_element_type=F32)
        acc_ref[...] = acc_ref[...] * jnp.concatenate([alpha] * (KV_RANK // LANES), axis=1) + pv

    @pl.when(j < i)
    def _():
        kv_tile_update(False)

    @pl.when(j == i)
    def _():
        kv_tile_update(True)
        inv = 1.0 / l_ref[...]
        o = acc_ref[...] * jnp.concatenate([inv] * (KV_RANK // LANES), axis=1)
        a = jnp.concatenate(
            [jnp.dot(o[h * t:(h + 1) * t].astype(BF16), wuv_ref[h], preferred_element_type=F32)
             for h in range(N_HEADS)], axis=1)
        merged = sga_ref[0].astype(F32) * a + gbb_ref[0].astype(F32)
        out_ref[0] = merged.astype(BF16)


def _attn(q, kcat, kmeta, sga, gbb, wts, t):
    B, L, _ = q.shape
    nq = L // t
    pairs = [(i, j) for i in range(nq) for j in range(i + 1)]
    qi = jnp.asarray([p[0] for p in pairs], jnp.int32)
    kj = jnp.asarray([p[1] for p in pairs], jnp.int32)
    rows = N_HEADS * t
    grid_spec = pltpu.PrefetchScalarGridSpec(
        num_scalar_prefetch=2,
        grid=(B, len(pairs)),
        in_specs=[
            pl.BlockSpec((1, t, Q_COLS), lambda b, s, qi, kj: (b, qi[s], 0)),
            pl.BlockSpec((1, t, KCAT_W), lambda b, s, qi, kj: (b, kj[s], 0)),
            pl.BlockSpec((LANES, KCAT_W), lambda b, s, qi, kj: (0, 0)),
            pl.BlockSpec((N_HEADS // 2, LANES, 2 * KV_RANK), lambda b, s, qi, kj: (0, 0, 0)),
            pl.BlockSpec((N_HEADS, KV_RANK, V_DIM), lambda b, s, qi, kj: (0, 0, 0)),
            pl.BlockSpec((1, t, D_MODEL), lambda b, s, qi, kj: (b, qi[s], 0)),
            pl.BlockSpec((1, t, D_MODEL), lambda b, s, qi, kj: (b, qi[s], 0)),
        ],
        out_specs=pl.BlockSpec((1, t, D_MODEL), lambda b, s, qi, kj: (b, qi[s], 0)),
        scratch_shapes=[
            pltpu.VMEM((rows, KCAT_W), BF16),
            pltpu.VMEM((rows, LANES), F32),
            pltpu.VMEM((rows, LANES), F32),
            pltpu.VMEM((rows, KV_RANK), F32),
        ],
    )
    return pl.pallas_call(
        functools.partial(_attn_kernel, t=t),
        out_shape=jax.ShapeDtypeStruct((B, L, D_MODEL), BF16),
        grid_spec=grid_spec,
        compiler_params=pltpu.CompilerParams(
            dimension_semantics=("arbitrary", "arbitrary"), vmem_limit_bytes=VMEM_LIMIT),
        name="attn",
    )(qi, kj, q, kcat, kmeta, wts["w_ukp"], wts["w_uv"], sga, gbb)


def _sfront_kernel(x_ref, g1_ref, w_ref, qg_ref, wuq_ref, kvg_ref, cos_ref, sin_ref, wpool_ref,
                   pscale_ref, hist_ref, wukp_ref,
                   qlat_ref, q1_ref, q2_ref, ckv_ref, kr_ref, p_ref, sga_ref, gbb_ref):
    n = x_ref.shape[0]
    x = x_ref[...]
    hb = _rms(x, g1_ref[...]).astype(BF16)
    z = jnp.dot(hb, w_ref[...], preferred_element_type=F32)
    cos = cos_ref[...]
    sin = sin_ref[...]

    qn = _rms(z[:, C_CQ:C_CQ + Q_RANK], qg_ref[...]).astype(BF16)
    q = jnp.dot(qn, wuq_ref[...], preferred_element_type=F32) * Q_SCALE
    q1 = q[:, Q_NOPE_W:Q_NOPE_W + LANES]
    q2 = q[:, Q_NOPE_W + LANES:]
    q1_ref[...] = q1 * cos - q2 * sin
    q2_ref[...] = q1 * sin + q2 * cos
    qb = q[:, 0:Q_NOPE_W].astype(BF16)
    for jp in range(N_HEADS // 2):
        ql = jnp.dot(qb[:, jp * LANES:(jp + 1) * LANES], wukp_ref[jp], preferred_element_type=F32)
        qlat_ref[2 * jp] = ql[:, 0:KV_RANK].astype(BF16)
        qlat_ref[2 * jp + 1] = ql[:, KV_RANK:].astype(BF16)

    ckv = _rms(z[:, C_CKV:C_CKV + KV_RANK], kvg_ref[...])
    ckv_ref[...] = ckv
    k1 = z[:, C_K1:C_K1 + LANES]
    k2 = z[:, C_K2:C_K2 + LANES]
    k1r = k1 * cos - k2 * sin
    k2r = k1 * sin + k2 * cos
    kr_ref[...] = _rope_pair_to_rows(k1r, k2r)

    p = z[:, C_P:C_P + POOL_WIDTH]
    p_ref[...] = p
    bs = []
    for g, w in enumerate(POOL_WINDOWS):
        lo = g * POOL_GROUP
        acc = p[:, lo:lo + POOL_GROUP]
        for k in range(1, w):
            acc = acc + hist_ref[POOL_HIST - k, :, lo:lo + POOL_GROUP]
        pooled = acc * (1.0 / w) - p[:, lo:lo + POOL_GROUP]
        bs.append(jnp.dot(pooled.astype(BF16), wpool_ref[g], preferred_element_type=F32))
    b = jnp.concatenate(bs, axis=1) * pscale_ref[...]
    sga_ref[...] = _sigmoid(z[:, C_GA:C_GA + D_MODEL]).astype(BF16)
    gbb_ref[...] = (_sigmoid(z[:, C_GB:C_GB + D_MODEL]) * b).astype(BF16)


def _sfront(xs, cos_s, sin_s, hist_t, wts):
    n = xs.shape[0]
    out_shape = (
        jax.ShapeDtypeStruct((N_HEADS, n, KV_RANK), BF16),
        jax.ShapeDtypeStruct((n, LANES), F32),
        jax.ShapeDtypeStruct((n, LANES), F32),
        jax.ShapeDtypeStruct((n, KV_RANK), F32),
        jax.ShapeDtypeStruct((n, ROPE_DIM), F32),
        jax.ShapeDtypeStruct((n, POOL_WIDTH), F32),
        jax.ShapeDtypeStruct((n, D_MODEL), BF16),
        jax.ShapeDtypeStruct((n, D_MODEL), BF16),
    )
    return pl.pallas_call(
        _sfront_kernel,
        out_shape=out_shape,
        compiler_params=pltpu.CompilerParams(vmem_limit_bytes=VMEM_LIMIT),
        name="sfront",
    )(xs, wts["g1"], wts["w_in"], wts["qg"], wts["w_uq"], wts["kvg"], cos_s, sin_s, wts["w_pool"],
      wts["pool_scale"], hist_t, wts["w_ukp"])


def _sattn_kernel(pt_ref, qlat_ref, qexp_ref, qrope_ref, ckvn_ref, krn_ref, cckv_hbm, ckrt_hbm,
                  o_ref, kbuf, rbuf, sem, m_ref, l_ref, acc_ref, *, pc, nc):
    b = pl.program_id(0)
    c = pl.program_id(1)
    nb = pl.num_programs(0)
    step = b * nc + c
    slot = step % 2
    nt_dims = (((1,), (1,)), ((), ()))

    def start_pages(bb, cc, sl):
        base = bb * (nc * pc) + cc * pc

        def body(pg, carry):
            phys = pt_ref[base + pg]
            pltpu.make_async_copy(cckv_hbm.at[phys], kbuf.at[sl, pg], sem.at[0, sl]).start()
            pltpu.make_async_copy(ckrt_hbm.at[phys], rbuf.at[sl, pg], sem.at[1, sl]).start()
            return carry

        lax.fori_loop(0, pc, body, 0)

    @pl.when(step == 0)
    def _():
        start_pages(b, c, slot)

    nxt = step + 1

    @pl.when(nxt < nb * nc)
    def _():
        start_pages(nxt // nc, nxt % nc, 1 - slot)

    ql = qlat_ref[0]

    @pl.when(c == 0)
    def _():
        cn = ckvn_ref[0]
        s0 = (jnp.sum(ql.astype(F32) * cn, axis=1, keepdims=True)
              + jnp.sum(qrope_ref[0] * krn_ref[0], axis=1, keepdims=True))
        m_ref[...] = jnp.broadcast_to(s0, (N_HEADS, LANES))
        l_ref[...] = jnp.ones((N_HEADS, LANES), F32)
        acc_ref[...] = jnp.broadcast_to(cn, (N_HEADS, KV_RANK))

    pltpu.make_async_copy(cckv_hbm.at[pl.ds(0, pc)], kbuf.at[slot], sem.at[0, slot]).wait()
    pltpu.make_async_copy(ckrt_hbm.at[pl.ds(0, pc)], rbuf.at[slot], sem.at[1, slot]).wait()

    kb = kbuf[slot].reshape(pc * PAGE_SIZE, KV_RANK).astype(BF16)
    s_lat = lax.dot_general(ql, kb, nt_dims, preferred_element_type=F32)
    qexp = qexp_ref[0]
    pieces = []
    for g in range(pc // ROPE_GROUP):
        r8 = rbuf[slot, g * ROPE_GROUP:(g + 1) * ROPE_GROUP].reshape(ROPE_GROUP * ROPE_DIM, PAGE_SIZE)
        og = jnp.dot(qexp, r8.astype(BF16), preferred_element_type=F32)
        pieces += [og[pp * N_HEADS:(pp + 1) * N_HEADS] for pp in range(ROPE_GROUP)]
    s = s_lat + jnp.concatenate(pieces, axis=1)
    m_prev = m_ref[...]
    m_new = jnp.maximum(m_prev, jnp.max(s, axis=1, keepdims=True))
    alpha = jnp.exp2(m_prev - m_new)
    p = jnp.exp2(s - m_new[:, 0:1])
    l_ref[...] = alpha * l_ref[...] + jnp.sum(p, axis=1, keepdims=True)
    m_ref[...] = m_new
    pv = jnp.dot(p.astype(BF16), kb, preferred_element_type=F32)
    acc_ref[...] = acc_ref[...] * jnp.concatenate([alpha] * (KV_RANK // LANES), axis=1) + pv

    @pl.when(c == nc - 1)
    def _():
        inv = 1.0 / l_ref[...]
        o_ref[0] = acc_ref[...] * jnp.concatenate([inv] * (KV_RANK // LANES), axis=1)


def _sattn(page_table, qlat, qexp, qrope, ckv_new, kr_new, cache_ckv, cache_krt, pc):
    n, n_pages = page_table.shape
    nc = n_pages // pc
    grid_spec = pltpu.PrefetchScalarGridSpec(
        num_scalar_prefetch=1,
        grid=(n, nc),
        in_specs=[
            pl.BlockSpec((1, N_HEADS, KV_RANK), lambda b, c, pt: (b, 0, 0)),
            pl.BlockSpec((1, ROPE_GROUP * N_HEADS, ROPE_GROUP * ROPE_DIM), lambda b, c, pt: (b, 0, 0)),
            pl.BlockSpec((1, N_HEADS, ROPE_DIM), lambda b, c, pt: (b, 0, 0)),
            pl.BlockSpec((1, 1, KV_RANK), lambda b, c, pt: (b, 0, 0)),
            pl.BlockSpec((1, 1, ROPE_DIM), lambda b, c, pt: (b, 0, 0)),
            pl.BlockSpec(memory_space=pl.ANY),
            pl.BlockSpec(memory_space=pl.ANY),
        ],
        out_specs=pl.BlockSpec((1, N_HEADS, KV_RANK), lambda b, c, pt: (b, 0, 0)),
        scratch_shapes=[
            pltpu.VMEM((2, pc, PAGE_SIZE, KV_RANK), F32),
            pltpu.VMEM((2, pc, ROPE_DIM, PAGE_SIZE), F32),
            pltpu.SemaphoreType.DMA((2, 2)),
            pltpu.VMEM((N_HEADS, LANES), F32),
            pltpu.VMEM((N_HEADS, LANES), F32),
            pltpu.VMEM((N_HEADS, KV_RANK), F32),
        ],
    )
    return pl.pallas_call(
        functools.partial(_sattn_kernel, pc=pc, nc=nc),
        out_shape=jax.ShapeDtypeStruct((n, N_HEADS, KV_RANK), F32),
        grid_spec=grid_spec,
        compiler_params=pltpu.CompilerParams(
            dimension_semantics=("arbitrary", "arbitrary"), vmem_limit_bytes=VMEM_LIMIT),
        name="sattn",
    )(page_table.reshape(-1), qlat, qexp, qrope, ckv_new, kr_new, cache_ckv, cache_krt)


def _smerge_kernel(o_ref, wuv_ref, sga_ref, gbb_ref, out_ref):
    a = jnp.concatenate(
        [jnp.dot(o_ref[h].astype(BF16), wuv_ref[h], preferred_element_type=F32)
         for h in range(N_HEADS)], axis=1)
    out_ref[...] = (sga_ref[...].astype(F32) * a + gbb_ref[...].astype(F32)).astype(BF16)


def _smerge(o_t, sga, gbb, wts):
    n = sga.shape[0]
    return pl.pallas_call(
        _smerge_kernel,
        out_shape=jax.ShapeDtypeStruct((n, D_MODEL), BF16),
        name="smerge",
    )(o_t, wts["w_uv"], sga, gbb)


def _post_kernel(x_ref, mg_ref, wout_ref, g2_ref, wr_ref, br_ref, x1_ref, h2_ref, ti_ref, tg_ref):
    tm = x_ref.shape[0]
    x1 = x_ref[...] + jnp.dot(mg_ref[...], wout_ref[...], preferred_element_type=F32)
    x1_ref[...] = x1
    h2 = _rms(x1, g2_ref[...])
    _store_row_tiles(h2_ref, h2)
    hh = h2.astype(BF16)
    hl = (h2 - hh.astype(F32)).astype(BF16)
    o1 = jnp.dot(hh, wr_ref[...], preferred_element_type=F32)
    o2 = jnp.dot(hl, wr_ref[:, 0:LANES], preferred_element_type=F32)
    logits = o1[:, 0:LANES] + o1[:, LANES:] + o2 + br_ref[...]
    lane = lax.broadcasted_iota(jnp.int32, (tm, LANES), 1)
    vals = logits
    tops, idxs = [], []
    for _ in range(TOP_K):
        m = jnp.max(vals, axis=1, keepdims=True)
        idx = jnp.min(jnp.where(vals == m, lane, LANES), axis=1, keepdims=True)
        tops.append(m)
        idxs.append(idx)
        vals = jnp.where(lane == idx, NEG_BIG * 2, vals)
    es = [jnp.exp(v - tops[0]) for v in tops]
    den = es[0] + es[1] + es[2] + es[3]
    ti = jnp.zeros((tm, LANES), jnp.int32)
    tg = jnp.zeros((tm, LANES), F32)
    for k in range(TOP_K):
        ti = jnp.where(lane == k, idxs[k], ti)
        tg = jnp.where(lane == k, es[k] / den, tg)
    ti_ref[...] = ti
    tg_ref[...] = tg


def _post(x, merged, wts, tm):
    T = x.shape[0]
    full = lambda shape: pl.BlockSpec(shape, lambda i: (0,) * len(shape))
    row = lambda w: pl.BlockSpec((tm, w), lambda i: (i, 0))
    return pl.pallas_call(
        _post_kernel,
        out_shape=(
            jax.ShapeDtypeStruct((T, D_MODEL), F32),
            jax.ShapeDtypeStruct((T * ROW_SUB, LANES), F32),
            jax.ShapeDtypeStruct((T, LANES), jnp.int32),
            jax.ShapeDtypeStruct((T, LANES), F32),
        ),
        grid=(T // tm,),
        in_specs=[row(D_MODEL), row(D_MODEL), full((D_MODEL, D_MODEL)), full((1, D_MODEL)),
                  full((D_MODEL, 2 * LANES)), full((1, LANES))],
        out_specs=(row(D_MODEL), pl.BlockSpec((tm * ROW_SUB, LANES), lambda i: (i, 0)),
                   row(LANES), row(LANES)),
        compiler_params=pltpu.CompilerParams(
            dimension_semantics=("arbitrary",), vmem_limit_bytes=VMEM_LIMIT),
        name="post",
    )(x, merged, wts["w_out"], wts["g2"], wts["w_r"], wts["b_r"])


def _dispatch_kernel(plan_ref, dest_ref, hp_ref, hs_ref, xs_hbm, hbuf, sem, psem, *, td, ntp, n_slots):
    i = pl.program_id(0)
    n = pl.num_programs(0)
    slot = i % 2

    def wait_tile(s):
        span = pl.ds(0, td * TOP_K * ROW_SUB)
        pltpu.make_async_copy(xs_hbm.at[span], xs_hbm.at[span], sem.at[s]).wait()

    @pl.when(i >= 2)
    def _():
        wait_tile(slot)

    @pl.when(i < ntp)
    def _():
        hbuf[slot] = hp_ref[...]

    @pl.when(i >= ntp)
    def _():
        hbuf[slot] = hs_ref[...]

    def body(r8, c):
        for rr in range(8):
            r = r8 * 8 + rr
            for k in range(TOP_K):
                d = dest_ref[0, 0, r * TOP_K + k]
                pltpu.make_async_copy(_row_tile(hbuf.at[slot], r), _row_tile(xs_hbm, d),
                                      sem.at[slot]).start(priority=k % 2)
        return c

    lax.fori_loop(0, td // 8, body, 0)

    @pl.when(i == n - 1)
    def _():
        def fill(s, c):
            pltpu.make_async_copy(_row_tile(hbuf.at[slot], 0), _row_tile(xs_hbm, s), psem.at[0]).start()
            return c

        def drain(s, c):
            pltpu.make_async_copy(_row_tile(hbuf.at[slot], 0), _row_tile(xs_hbm, s), psem.at[0]).wait()
            return c

        def per_expert(e, c):
            lo = plan_ref[e]
            hi = plan_ref[N_EXPERTS + e]
            lax.fori_loop(lo, hi, fill, 0)
            lax.fori_loop(lo, hi, drain, 0)
            return c

        lax.fori_loop(0, N_EXPERTS, per_expert, 0)

        def tail_copy(c):
            rows = pl.ds(pl.multiple_of(c * (td * ROW_SUB), td * ROW_SUB), td * ROW_SUB)
            return pltpu.make_async_copy(hbuf.at[slot], xs_hbm.at[rows], psem.at[0])

        def fill_tail(c, cc):
            tail_copy(c).start()
            return cc

        def drain_tail(c, cc):
            tail_copy(c).wait()
            return cc

        first_tail = lax.shift_right_logical(plan_ref[2 * N_EXPERTS - 1], td.bit_length() - 1)
        lax.fori_loop(first_tail, n_slots // td, fill_tail, 0)
        lax.fori_loop(first_tail, n_slots // td, drain_tail, 0)
        wait_tile(slot)

        @pl.when(n >= 2)
        def _():
            wait_tile(1 - slot)


def _dispatch(plan, dest, h2_p, h2_s, cap, td):
    T = dest.shape[0]
    nt = T // td
    ntp = h2_p.shape[0] // (td * ROW_SUB)
    tile_spec = lambda f: pl.BlockSpec((td * ROW_SUB, LANES), f)
    grid_spec = pltpu.PrefetchScalarGridSpec(
        num_scalar_prefetch=1,
        grid=(nt,),
        in_specs=[
            pl.BlockSpec((1, 1, td * TOP_K), lambda i, plan: (i, 0, 0), memory_space=pltpu.SMEM),
            tile_spec(lambda i, plan: (jnp.minimum(i, ntp - 1), 0)),
            tile_spec(lambda i, plan: (jnp.maximum(i - ntp, 0), 0)),
        ],
        out_specs=pl.BlockSpec(memory_space=pl.ANY),
        scratch_shapes=[pltpu.VMEM((2, td * ROW_SUB, LANES), F32), pltpu.SemaphoreType.DMA((2,)),
                        pltpu.SemaphoreType.DMA((1,))],
    )
    return pl.pallas_call(
        functools.partial(_dispatch_kernel, td=td, ntp=ntp, n_slots=cap),
        out_shape=jax.ShapeDtypeStruct((cap * ROW_SUB, LANES), F32),
        grid_spec=grid_spec,
        compiler_params=pltpu.CompilerParams(dimension_semantics=("arbitrary",)),
        name="dispatch",
    )(plan, dest.reshape(nt, 1, td * TOP_K), h2_p, h2_s)


def _experts_kernel(te_ref, tf_ref, tv_ref, ts_ref, xs_ref, wgu_ref, bgu_ref, wdn_ref, bdn_ref, out_ref,
                    wgu_bf, wdn_bf):
    t = pl.program_id(0)

    @pl.when(tf_ref[t] == 1)
    def _():
        wgu_bf[...] = wgu_ref[0].astype(BF16)
        wdn_bf[...] = wdn_ref[0].astype(BF16)

    @pl.when(tv_ref[t] == 1)
    def _():
        x = _load_row_tiles(xs_ref, xs_ref.shape[0] // ROW_SUB).astype(BF16)
        gu = jnp.dot(x, wgu_bf[...], preferred_element_type=F32) + bgu_ref[0]
        gate = jnp.minimum(gu[:, 0:D_FF], SWIGLU_LIMIT)
        up = jnp.clip(gu[:, D_FF:], -SWIGLU_LIMIT, SWIGLU_LIMIT)
        act = gate * _sigmoid(gate * SWIGLU_ALPHA)
        hmid = ((up + 1.0) * act).astype(BF16)
        _store_row_tiles(out_ref, jnp.dot(hmid, wdn_bf[...], preferred_element_type=F32) + bdn_ref[0])

    @pl.when(tv_ref[t] == 0)
    def _():
        out_ref[...] = jnp.zeros_like(out_ref)


def _experts(tile_e, tile_first, tile_valid, tile_src, xs, w_gate_up, b_gate_up, w_down, b_down, tm):
    cap = xs.shape[0] // ROW_SUB
    n_tiles = cap // tm
    grid_spec = pltpu.PrefetchScalarGridSpec(
        num_scalar_prefetch=4,
        grid=(n_tiles,),
        in_specs=[
            pl.BlockSpec((tm * ROW_SUB, LANES), lambda t, te, tf, tv, ts: (ts[t], 0)),
            pl.BlockSpec((1, D_MODEL, 2 * D_FF), lambda t, te, tf, tv, ts: (te[t], 0, 0)),
            pl.BlockSpec((1, 1, 2 * D_FF), lambda t, te, tf, tv, ts: (te[t], 0, 0)),
            pl.BlockSpec((1, D_FF, D_MODEL), lambda t, te, tf, tv, ts: (te[t], 0, 0)),
            pl.BlockSpec((1, 1, D_MODEL), lambda t, te, tf, tv, ts: (te[t], 0, 0)),
        ],
        out_specs=pl.BlockSpec((tm * ROW_SUB, LANES), lambda t, te, tf, tv, ts: (t, 0)),
        scratch_shapes=[pltpu.VMEM((D_MODEL, 2 * D_FF), BF16), pltpu.VMEM((D_FF, D_MODEL), BF16)],
    )
    return pl.pallas_call(
        _experts_kernel,
        out_shape=jax.ShapeDtypeStruct((cap * ROW_SUB, LANES), F32),
        grid_spec=grid_spec,
        compiler_params=pltpu.CompilerParams(
            dimension_semantics=("arbitrary",), vmem_limit_bytes=VMEM_LIMIT),
        name="experts",
    )(tile_e, tile_first, tile_valid, tile_src, xs, w_gate_up, b_gate_up.reshape(N_EXPERTS, 1, 2 * D_FF),
      w_down, b_down.reshape(N_EXPERTS, 1, D_MODEL))


def _combine_kernel(dcur_ref, dnxt_ref, x1_ref, tg_ref, gf_ref, yp_hbm, out_ref, buf, sem, *, tm):
    i = pl.program_id(0)
    n = pl.num_programs(0)
    slot = i % 2

    def start_rows(dref, sl):
        def body(r8, c):
            for rr in range(8):
                r = r8 * 8 + rr
                for k in range(TOP_K):
                    d = dref[0, 0, r * TOP_K + k]
                    pltpu.make_async_copy(_row_tile(yp_hbm, d), _row_tile(buf.at[sl], k * tm + r),
                                          sem.at[sl]).start(priority=k % 2)
            return c

        lax.fori_loop(0, tm // 8, body, 0)

    @pl.when(i == 0)
    def _():
        start_rows(dcur_ref, slot)

    @pl.when(i + 1 < n)
    def _():
        start_rows(dnxt_ref, 1 - slot)

    pltpu.make_async_copy(yp_hbm.at[pl.ds(0, TOP_K * tm * ROW_SUB)], buf.at[slot], sem.at[slot]).wait()
    y = x1_ref[...]
    tg = tg_ref[...]
    for k in range(TOP_K):
        y = y + _load_row_tiles(buf.at[slot], tm, first=k * tm) * tg[:, k:k + 1]
    out_ref[...] = _rms(y, gf_ref[...])


def _combine(dest, x1, tg, gf, yp, tm):
    T = x1.shape[0]
    nt = T // tm
    dest3 = dest.reshape(nt, 1, tm * TOP_K)
    return pl.pallas_call(
        functools.partial(_combine_kernel, tm=tm),
        out_shape=jax.ShapeDtypeStruct((T, D_MODEL), F32),
        grid=(nt,),
        in_specs=[
            pl.BlockSpec((1, 1, tm * TOP_K), lambda i: (i, 0, 0), memory_space=pltpu.SMEM),
            pl.BlockSpec((1, 1, tm * TOP_K), lambda i: (jnp.minimum(i + 1, nt - 1), 0, 0),
                         memory_space=pltpu.SMEM),
            pl.BlockSpec((tm, D_MODEL), lambda i: (i, 0)),
            pl.BlockSpec((tm, LANES), lambda i: (i, 0)),
            pl.BlockSpec((1, D_MODEL), lambda i: (0, 0)),
            pl.BlockSpec(memory_space=pl.ANY),
        ],
        out_specs=pl.BlockSpec((tm, D_MODEL), lambda i: (i, 0)),
        scratch_shapes=[pltpu.VMEM((2, TOP_K * tm * ROW_SUB, LANES), F32), pltpu.SemaphoreType.DMA((2,))],
        compiler_params=pltpu.CompilerParams(
            dimension_semantics=("arbitrary",), vmem_limit_bytes=VMEM_LIMIT),
        name="combine",
    )(dest3, dest3, x1, tg, gf, yp)


def _rope_tables(pos):
    inv = ROPE_BASE ** (-jnp.arange(ROPE_HALF, dtype=F32) / ROPE_HALF)
    ang = pos.astype(F32)[:, None] * inv[None, :]
    return jnp.tile(jnp.cos(ang), (1, N_HEADS)), jnp.tile(jnp.sin(ang), (1, N_HEADS))


def _prep_weights(norm1_g, w_in, q_norm_g, w_uq, kv_norm_g, w_uk, w_uv, w_pool, pool_scale, w_out,
                  norm2_g, w_router, b_router):
    i0 = Q_RANK
    i1 = i0 + KV_RANK
    i2 = i1 + ROPE_DIM
    i3 = i2 + POOL_WIDTH
    w_packed = jnp.concatenate([
        w_in[:, 0:i1],
        jnp.tile(w_in[:, i1:i1 + ROPE_HALF], (1, N_HEADS)),
        jnp.tile(w_in[:, i1 + ROPE_HALF:i2], (1, N_HEADS)),
        w_in[:, i2:],
    ], axis=1).astype(BF16)
    wq = w_uq.reshape(Q_RANK, N_HEADS, QK_DIM)
    wq_packed = jnp.concatenate([
        wq[:, :, 0:NOPE_DIM].reshape(Q_RANK, Q_NOPE_W),
        wq[:, :, NOPE_DIM:NOPE_DIM + ROPE_HALF].reshape(Q_RANK, LANES),
        wq[:, :, NOPE_DIM + ROPE_HALF:].reshape(Q_RANK, LANES),
    ], axis=1).astype(BF16)
    zeros = jnp.zeros((NOPE_DIM, KV_RANK), F32)
    w_ukp = jnp.stack([
        jnp.concatenate([
            jnp.concatenate([w_uk[2 * jp], zeros], axis=1),
            jnp.concatenate([zeros, w_uk[2 * jp + 1]], axis=1)], axis=0)
        for jp in range(N_HEADS // 2)]).astype(BF16)
    wr_hi = w_router.astype(BF16)
    wr_lo = (w_router - wr_hi.astype(F32)).astype(BF16)
    pad = ((0, 0), (0, LANES - N_EXPERTS))
    w_r = jnp.concatenate([jnp.pad(wr_hi, pad), jnp.pad(wr_lo, pad)], axis=1)
    b_r = jnp.concatenate([b_router.astype(F32), jnp.full((LANES - N_EXPERTS,), NEG_BIG, F32)])[None, :]
    return {
        "g1": norm1_g[None, :], "w_in": w_packed, "qg": q_norm_g[None, :], "w_uq": wq_packed,
        "kvg": kv_norm_g[None, :], "w_ukp": w_ukp, "w_uv": w_uv.astype(BF16),
        "w_pool": w_pool.astype(BF16), "pool_scale": pool_scale[None, :], "w_out": w_out.astype(BF16),
        "g2": norm2_g[None, :], "w_r": w_r, "b_r": b_r,
    }


def _route(top_i, tm):
    T = top_i.shape[0]
    n_asg = T * TOP_K
    n_tiles = -(-n_asg // tm) + N_EXPERTS
    e = top_i.reshape(-1)
    onehot = (jnp.arange(N_EXPERTS, dtype=jnp.int32)[:, None] == e[None, :]).astype(jnp.int32)
    csum = jnp.cumsum(onehot, axis=1)
    counts = csum[:, -1]
    padded = (counts + tm - 1) // tm * tm
    pend = jnp.cumsum(padded)
    pstart = pend - padded
    dest = jnp.sum(onehot * (csum - 1 + pstart[:, None]), axis=0)
    tile_start = jnp.arange(n_tiles, dtype=jnp.int32) * tm
    tile_e = jnp.minimum(jnp.sum((pend[None, :] <= tile_start[:, None]).astype(jnp.int32), axis=1), N_EXPERTS - 1)
    n_used = pend[-1] // tm
    tile_valid = (tile_start < pend[-1]).astype(jnp.int32)
    tile_first = jnp.concatenate([jnp.ones((1,), jnp.int32), (tile_e[1:] != tile_e[:-1]).astype(jnp.int32)])
    tile_src = jnp.minimum(jnp.arange(n_tiles, dtype=jnp.int32), n_used - 1)
    plan = jnp.concatenate([pstart + counts, pend]).astype(jnp.int32)
    return dest.reshape(T, TOP_K).astype(jnp.int32), plan, tile_e.astype(jnp.int32), tile_first, tile_valid, tile_src


def kernel(x_prompt, x_sample, cache_ckv, cache_krope, state_pool, page_table, meta_tokens, norm1_g, w_in, q_norm_g, w_uq, kv_norm_g, w_uk, w_uv, w_pool, pool_scale, w_out, norm2_g, w_router, b_router, w_gate_up, b_gate_up, w_down, b_down, final_norm_g):
    B, L, _ = x_prompt.shape
    n = x_sample.shape[0]
    n_pages = page_table.shape[1]
    past_len = n_pages * PAGE_SIZE
    wts = _prep_weights(norm1_g[0], w_in[0], q_norm_g[0], w_uq[0], kv_norm_g[0], w_uk[0], w_uv[0],
                        w_pool[0], pool_scale[0], w_out[0], norm2_g[0], w_router[0], b_router[0])

    cos_m, sin_m = _rope_tables(jnp.arange(N_META))
    _, kcat_m, ckv_m, kr_m, p_m, _, _ = _front(
        meta_tokens[None], cos_m, sin_m, jnp.zeros((16, POOL_WIDTH), F32), wts, N_META)

    cos_p, sin_p = _rope_tables(N_META + jnp.arange(L))
    q, kcat, ckv_p, kr_p, ptail, sga, gbb = _front(x_prompt, cos_p, sin_p, p_m[0], wts, FRONT_TM)
    kmeta = jnp.pad(kcat_m[0], ((0, LANES - N_META), (0, 0)))
    merged_p = _attn(q, kcat, kmeta, sga, gbb, wts, ATTN_T)

    cos_s, sin_s = _rope_tables(jnp.full((n,), past_len))
    hist_t = jnp.transpose(state_pool[0], (1, 0, 2))
    xs = x_sample[:, 0, :]
    qlat_t, q1_s, q2_s, ckv_s, kr_s, p_s, sga_s, gbb_s = _sfront(xs, cos_s, sin_s, hist_t, wts)
    qlat_s = jnp.transpose(qlat_t, (1, 0, 2))
    qrope_s = jnp.concatenate([q1_s.reshape(n, N_HEADS, ROPE_HALF), q2_s.reshape(n, N_HEADS, ROPE_HALF)], axis=2)
    blk_r = jnp.arange(ROPE_GROUP * N_HEADS, dtype=jnp.int32)[:, None] // N_HEADS
    blk_c = jnp.arange(ROPE_GROUP * ROPE_DIM, dtype=jnp.int32)[None, :] // ROPE_DIM
    qexp_s = jnp.where(blk_r == blk_c, jnp.tile(qrope_s, (1, ROPE_GROUP, ROPE_GROUP)), 0.0).astype(BF16)
    o_s = _sattn(page_table, qlat_s, qexp_s, qrope_s, ckv_s[:, None, :], kr_s[:, None, :],
                 cache_ckv[0], jnp.swapaxes(cache_krope[0], 1, 2), min(SATTN_PAGES, n_pages))
    merged_s = _smerge(jnp.transpose(o_s, (1, 0, 2)), sga_s, gbb_s, wts)

    Tp = B * L
    x1_p, h2_p, ti_p, tg_p = _post(x_prompt.reshape(Tp, D_MODEL), merged_p.reshape(Tp, D_MODEL), wts, POST_TM)
    x1_s, h2_s, ti_s, tg_s = _post(xs, merged_s, wts, n)
    top_i = jnp.concatenate([ti_p[:, 0:TOP_K], ti_s[:, 0:TOP_K]], axis=0)
    dest, plan, tile_e, tile_first, tile_valid, tile_src = _route(top_i, MOE_TM)
    cap = tile_e.shape[0] * MOE_TM
    xs_g = _dispatch(plan, dest, h2_p, h2_s, cap, DISPATCH_TD)
    yp = _experts(tile_e, tile_first, tile_valid, tile_src, xs_g, w_gate_up[0], b_gate_up[0], w_down[0],
                  b_down[0], MOE_TM)
    gf = final_norm_g[None, :]
    y_prompt = _combine(dest[:Tp], x1_p, tg_p, gf, yp, POST_TM).reshape(B, L, D_MODEL)
    y_sample = _combine(dest[Tp:], x1_s, tg_s, gf, yp, n).reshape(n, 1, D_MODEL)
    new_ckv_prompt = jnp.concatenate([jnp.broadcast_to(ckv_m, (B, N_META, KV_RANK)), ckv_p], axis=1)[None]
    new_krope_prompt = jnp.concatenate([jnp.broadcast_to(kr_m, (B, N_META, ROPE_DIM)), kr_p], axis=1)[None]
    new_pool_prompt = ptail[:, 16 - POOL_HIST:, :][None]
    new_ckv_sample = ckv_s[None, :, None, :]
    new_krope_sample = kr_s[None, :, None, :]
    new_pool_sample = jnp.concatenate([state_pool[0][:, 1:, :], p_s[:, None, :]], axis=1)[None]
    return (y_prompt, y_sample, new_ckv_prompt, new_krope_prompt, new_pool_prompt,
            new_ckv_sample, new_krope_sample, new_pool_sample)
```

```python
import functools
import math

import jax
import jax.numpy as jnp
from jax import lax
from jax.experimental import pallas as pl
from jax.experimental.pallas import tpu as pltpu

F32 = jnp.float32
BF16 = jnp.bfloat16

D_MODEL = 1024
N_META = 16
N_HEADS = 8
Q_RANK = 256
KV_RANK = 256
NOPE_DIM = 64
ROPE_DIM = 32
ROPE_HALF = ROPE_DIM // 2
QK_DIM = NOPE_DIM + ROPE_DIM
V_DIM = D_MODEL // N_HEADS
ROPE_BASE = 10000.0
ATTN_SCALE = 1.0 / math.sqrt(QK_DIM)
Q_SCALE = ATTN_SCALE * math.log2(math.e)
POOL_WIDTH = D_MODEL // 2
POOL_WINDOWS = (2, 4, 8, 16)
POOL_GROUP = POOL_WIDTH // len(POOL_WINDOWS)
POOL_OUT_GROUP = D_MODEL // len(POOL_WINDOWS)
POOL_HIST = max(POOL_WINDOWS) - 1
N_EXPERTS = 32
TOP_K = 4
D_FF = D_MODEL
SWIGLU_LIMIT = 7.0
SWIGLU_ALPHA = 1.702
EPS = 1e-6
PAGE_SIZE = 128

LANES = 128
ROW_SUB = D_MODEL // LANES
NEG_BIG = -1e30

C_CQ = 0
C_CKV = C_CQ + Q_RANK
C_K1 = C_CKV + KV_RANK
C_K2 = C_K1 + LANES
C_P = C_K2 + LANES
C_GA = C_P + POOL_WIDTH
C_GB = C_GA + D_MODEL
C_END = C_GB + D_MODEL
Q_NOPE_W = N_HEADS * NOPE_DIM
Q_COLS = Q_NOPE_W + 2 * LANES
KCAT_W = KV_RANK + 2 * LANES

FRONT_TM = 512
ATTN_T = 512
POST_TM = 512
MOE_TM = 512
SATTN_PAGES = 128
ROPE_GROUP = 2 * LANES // ROPE_DIM
DISPATCH_TD = 128
VMEM_LIMIT = 56 * 1024 * 1024


def _rms(x, g):
    return x * lax.rsqrt(jnp.mean(x * x, axis=-1, keepdims=True) + EPS) * g


def _sigmoid(x):
    return 1.0 / (1.0 + jnp.exp(-x))


def _store_row_tiles(ref, x):
    n = x.shape[0]
    for s in range(ROW_SUB):
        ref[pl.ds(s, n, stride=ROW_SUB), :] = x[:, s * LANES:(s + 1) * LANES]


def _load_row_tiles(ref, n, first=0):
    return jnp.concatenate(
        [ref[pl.ds(first * ROW_SUB + s, n, stride=ROW_SUB), :] for s in range(ROW_SUB)], axis=1)


def _row_tile(ref, r):
    return ref.at[pl.ds(pl.multiple_of(r * ROW_SUB, ROW_SUB), ROW_SUB)]


def _rope_pair_to_rows(k1r, k2r):
    lane = lax.broadcasted_iota(jnp.int32, k1r.shape, 1)
    return jnp.where(lane < ROPE_HALF, k1r, pltpu.roll(k2r, ROPE_HALF, axis=1))[:, 0:ROPE_DIM]


def _front_kernel(x_ref, g1_ref, w_ref, qg_ref, wuq_ref, kvg_ref, cos_ref, sin_ref, wpool_ref,
                  pscale_ref, pprev_ref,
                  q_ref, kcat_ref, ckv_ref, kr_ref, ptail_ref, sga_ref, gbb_ref,
                  pext_ref, *, tm):
    i = pl.program_id(1)
    x = x_ref[0]
    hb = _rms(x, g1_ref[...]).astype(BF16)
    z = jnp.dot(hb, w_ref[...], preferred_element_type=F32)
    cos = cos_ref[...]
    sin = sin_ref[...]

    qn = _rms(z[:, C_CQ:C_CQ + Q_RANK], qg_ref[...]).astype(BF16)
    q = jnp.dot(qn, wuq_ref[...], preferred_element_type=F32) * Q_SCALE
    q1 = q[:, Q_NOPE_W:Q_NOPE_W + LANES]
    q2 = q[:, Q_NOPE_W + LANES:]
    q_ref[0, :, 0:Q_NOPE_W] = q[:, 0:Q_NOPE_W].astype(BF16)
    q_ref[0, :, Q_NOPE_W:Q_NOPE_W + LANES] = (q1 * cos - q2 * sin).astype(BF16)
    q_ref[0, :, Q_NOPE_W + LANES:] = (q1 * sin + q2 * cos).astype(BF16)

    ckv = _rms(z[:, C_CKV:C_CKV + KV_RANK], kvg_ref[...])
    ckv_ref[0] = ckv
    k1 = z[:, C_K1:C_K1 + LANES]
    k2 = z[:, C_K2:C_K2 + LANES]
    k1r = k1 * cos - k2 * sin
    k2r = k1 * sin + k2 * cos
    kcat_ref[0, :, 0:KV_RANK] = ckv.astype(BF16)
    kcat_ref[0, :, KV_RANK:KV_RANK + LANES] = k1r.astype(BF16)
    kcat_ref[0, :, KV_RANK + LANES:] = k2r.astype(BF16)
    kr_ref[0] = _rope_pair_to_rows(k1r, k2r)

    p = z[:, C_P:C_P + POOL_WIDTH]

    @pl.when(i == 0)
    def _():
        pext_ref[0:16, :] = pprev_ref[...]

    @pl.when(i > 0)
    def _():
        pext_ref[0:16, :] = pext_ref[tm:tm + 16, :]

    pext_ref[16:16 + tm, :] = p
    ptail_ref[0] = p[tm - 16:tm, :]
    bs = []
    for g, w in enumerate(POOL_WINDOWS):
        lo = g * POOL_GROUP
        acc = pext_ref[16:16 + tm, lo:lo + POOL_GROUP]
        for k in range(1, w):
            acc = acc + pext_ref[16 - k:16 - k + tm, lo:lo + POOL_GROUP]
        pooled = acc * (1.0 / w) - p[:, lo:lo + POOL_GROUP]
        bs.append(jnp.dot(pooled.astype(BF16), wpool_ref[g], preferred_element_type=F32))
    b = jnp.concatenate(bs, axis=1) * pscale_ref[...]
    sga_ref[0] = _sigmoid(z[:, C_GA:C_GA + D_MODEL]).astype(BF16)
    gbb_ref[0] = (_sigmoid(z[:, C_GB:C_GB + D_MODEL]) * b).astype(BF16)


def _front(x, cos_t, sin_t, pprev, wts, tm):
    B, L, _ = x.shape
    nt = L // tm
    full = lambda shape: pl.BlockSpec(shape, lambda b, i: (0,) * len(shape))
    row = lambda w: pl.BlockSpec((1, tm, w), lambda b, i: (b, i, 0))
    out_shape = (
        jax.ShapeDtypeStruct((B, L, Q_COLS), BF16),
        jax.ShapeDtypeStruct((B, L, KCAT_W), BF16),
        jax.ShapeDtypeStruct((B, L, KV_RANK), F32),
        jax.ShapeDtypeStruct((B, L, ROPE_DIM), F32),
        jax.ShapeDtypeStruct((B, 16, POOL_WIDTH), F32),
        jax.ShapeDtypeStruct((B, L, D_MODEL), BF16),
        jax.ShapeDtypeStruct((B, L, D_MODEL), BF16),
    )
    return pl.pallas_call(
        functools.partial(_front_kernel, tm=tm),
        out_shape=out_shape,
        grid=(B, nt),
        in_specs=[
            row(D_MODEL),
            full((1, D_MODEL)),
            full((D_MODEL, C_END)),
            full((1, Q_RANK)),
            full((Q_RANK, Q_COLS)),
            full((1, KV_RANK)),
            pl.BlockSpec((tm, LANES), lambda b, i: (i, 0)),
            pl.BlockSpec((tm, LANES), lambda b, i: (i, 0)),
            full((len(POOL_WINDOWS), POOL_GROUP, POOL_OUT_GROUP)),
            full((1, D_MODEL)),
            full((16, POOL_WIDTH)),
        ],
        out_specs=(
            row(Q_COLS), row(KCAT_W), row(KV_RANK), row(ROPE_DIM),
            pl.BlockSpec((1, 16, POOL_WIDTH), lambda b, i: (b, 0, 0)),
            row(D_MODEL), row(D_MODEL),
        ),
        scratch_shapes=[pltpu.VMEM((tm + 16, POOL_WIDTH), F32)],
        compiler_params=pltpu.CompilerParams(
            dimension_semantics=("arbitrary", "arbitrary"), vmem_limit_bytes=VMEM_LIMIT),
        name="front",
    )(x, wts["g1"], wts["w_in"], wts["qg"], wts["w_uq"], wts["kvg"], cos_t, sin_t, wts["w_pool"],
      wts["pool_scale"], pprev)


def _build_qcat(q, wukp_ref, qcat_ref, t):
    lane = lax.broadcasted_iota(jnp.int32, (t, LANES), 1)
    q1 = q[:, Q_NOPE_W:Q_NOPE_W + LANES].astype(F32)
    q2 = q[:, Q_NOPE_W + LANES:].astype(F32)
    zero = jnp.zeros_like(q1)
    for jp in range(N_HEADS // 2):
        ql = jnp.dot(q[:, jp * LANES:(jp + 1) * LANES], wukp_ref[jp], preferred_element_type=F32)
        for s in range(2):
            h = 2 * jp + s
            sel = (lane >= h * ROPE_HALF) & (lane < (h + 1) * ROPE_HALF)
            qcat_ref[h * t:(h + 1) * t, 0:KV_RANK] = ql[:, s * KV_RANK:(s + 1) * KV_RANK].astype(BF16)
            qcat_ref[h * t:(h + 1) * t, KV_RANK:KV_RANK + LANES] = jnp.where(sel, q1, zero).astype(BF16)
            qcat_ref[h * t:(h + 1) * t, KV_RANK + LANES:] = jnp.where(sel, q2, zero).astype(BF16)


def _attn_kernel(qi_ref, kj_ref, q_ref, kcat_ref, kmeta_ref, wukp_ref, wuv_ref, sga_ref, gbb_ref,
                 out_ref, qcat_ref, m_ref, l_ref, acc_ref, *, t):
    s_id = pl.program_id(1)
    i = qi_ref[s_id]
    j = kj_ref[s_id]
    rows = N_HEADS * t
    nt_dims = (((1,), (1,)), ((), ()))

    @pl.when(j == 0)
    def _():
        _build_qcat(q_ref[0], wukp_ref, qcat_ref, t)
        km = kmeta_ref[...]
        s = lax.dot_general(qcat_ref[...], km, nt_dims, preferred_element_type=F32)
        col = lax.broadcasted_iota(jnp.int32, (rows, LANES), 1)
        s = jnp.where(col < N_META, s, NEG_BIG)
        m = jnp.max(s, axis=1, keepdims=True)
        p = jnp.exp2(s - m)
        m_ref[...] = jnp.broadcast_to(m, (rows, LANES))
        l_ref[...] = jnp.broadcast_to(jnp.sum(p, axis=1, keepdims=True), (rows, LANES))
        acc_ref[...] = jnp.dot(p.astype(BF16), km[:, 0:KV_RANK], preferred_element_type=F32)

    def kv_tile_update(on_diagonal):
        k = kcat_ref[0]
        s = lax.dot_general(qcat_ref[...], k, nt_dims, preferred_element_type=F32)
        if on_diagonal:
            r_tok = lax.broadcasted_iota(jnp.int32, (rows, t), 0) & (t - 1)
            col = lax.broadcasted_iota(jnp.int32, (rows, t), 1)
            s = jnp.where(col <= r_tok, s, NEG_BIG)
        m_prev = m_ref[...]
        m_new = jnp.maximum(m_prev, jnp.max(s, axis=1, keepdims=True))
        alpha = jnp.exp2(m_prev - m_new)
        p = jnp.exp2(s - jnp.concatenate([m_new] * (t // LANES), axis=1))
        l_ref[...] = alpha * l_ref[...] + jnp.sum(p, axis=1, keepdims=True)
        m_ref[...] = m_new
        pv = jnp.dot(p.astype(BF16), k[:, 0:KV_RANK], preferred_element_type=F32)
        acc_ref[...] = acc_ref[...] * jnp.concatenate([alpha] * (KV_RANK // LANES), axis=1) + pv

    @pl.when(j < i)
    def _():
        kv_tile_update(False)

    @pl.when(j == i)
    def _():
        kv_tile_update(True)
        inv = 1.0 / l_ref[...]
        o = acc_ref[...] * jnp.concatenate([inv] * (KV_RANK // LANES), axis=1)
        a = jnp.concatenate(
            [jnp.dot(o[h * t:(h + 1) * t].astype(BF16), wuv_ref[h], preferred_element_type=F32)
             for h in range(N_HEADS)], axis=1)
        merged = sga_ref[0].astype(F32) * a + gbb_ref[0].astype(F32)
        out_ref[0] = merged.astype(BF16)


def _attn(q, kcat, kmeta, sga, gbb, wts, t):
    B, L, _ = q.shape
    nq = L // t
    pairs = [(i, j) for i in range(nq) for j in range(i + 1)]
    qi = jnp.asarray([p[0] for p in pairs], jnp.int32)
    kj = jnp.asarray([p[1] for p in pairs], jnp.int32)
    rows = N_HEADS * t
    grid_spec = pltpu.PrefetchScalarGridSpec(
        num_scalar_prefetch=2,
        grid=(B, len(pairs)),
        in_specs=[
            pl.BlockSpec((1, t, Q_COLS), lambda b, s, qi, kj: (b, qi[s], 0)),
            pl.BlockSpec((1, t, KCAT_W), lambda b, s, qi, kj: (b, kj[s], 0)),
            pl.BlockSpec((LANES, KCAT_W), lambda b, s, qi, kj: (0, 0)),
            pl.BlockSpec((N_HEADS // 2, LANES, 2 * KV_RANK), lambda b, s, qi, kj: (0, 0, 0)),
            pl.BlockSpec((N_HEADS, KV_RANK, V_DIM), lambda b, s, qi, kj: (0, 0, 0)),
            pl.BlockSpec((1, t, D_MODEL), lambda b, s, qi, kj: (b, qi[s], 0)),
            pl.BlockSpec((1, t, D_MODEL), lambda b, s, qi, kj: (b, qi[s], 0)),
        ],
        out_specs=pl.BlockSpec((1, t, D_MODEL), lambda b, s, qi, kj: (b, qi[s], 0)),
        scratch_shapes=[
            pltpu.VMEM((rows, KCAT_W), BF16),
            pltpu.VMEM((rows, LANES), F32),
            pltpu.VMEM((rows, LANES), F32),
            pltpu.VMEM((rows, KV_RANK), F32),
        ],
    )
    return pl.pallas_call(
        functools.partial(_attn_kernel, t=t),
        out_shape=jax.ShapeDtypeStruct((B, L, D_MODEL), BF16),
        grid_spec=grid_spec,
        compiler_params=pltpu.CompilerParams(
            dimension_semantics=("arbitrary", "arbitrary"), vmem_limit_bytes=VMEM_LIMIT),
        name="attn",
    )(qi, kj, q, kcat, kmeta, wts["w_ukp"], wts["w_uv"], sga, gbb)


def _sfront_kernel(x_ref, g1_ref, w_ref, qg_ref, wuq_ref, kvg_ref, cos_ref, sin_ref, wpool_ref,
                   pscale_ref, hist_ref, wukp_ref,
                   qlat_ref, q1_ref, q2_ref, ckv_ref, kr_ref, p_ref, sga_ref, gbb_ref):
    n = x_ref.shape[0]
    x = x_ref[...]
    hb = _rms(x, g1_ref[...]).astype(BF16)
    z = jnp.dot(hb, w_ref[...], preferred_element_type=F32)
    cos = cos_ref[...]
    sin = sin_ref[...]

    qn = _rms(z[:, C_CQ:C_CQ + Q_RANK], qg_ref[...]).astype(BF16)
    q = jnp.dot(qn, wuq_ref[...], preferred_element_type=F32) * Q_SCALE
    q1 = q[:, Q_NOPE_W:Q_NOPE_W + LANES]
    q2 = q[:, Q_NOPE_W + LANES:]
    q1_ref[...] = q1 * cos - q2 * sin
    q2_ref[...] = q1 * sin + q2 * cos
    qb = q[:, 0:Q_NOPE_W].astype(BF16)
    for jp in range(N_HEADS // 2):
        ql = jnp.dot(qb[:, jp * LANES:(jp + 1) * LANES], wukp_ref[jp], preferred_element_type=F32)
        qlat_ref[2 * jp] = ql[:, 0:KV_RANK].astype(BF16)
        qlat_ref[2 * jp + 1] = ql[:, KV_RANK:].astype(BF16)

    ckv = _rms(z[:, C_CKV:C_CKV + KV_RANK], kvg_ref[...])
    ckv_ref[...] = ckv
    k1 = z[:, C_K1:C_K1 + LANES]
    k2 = z[:, C_K2:C_K2 + LANES]
    k1r = k1 * cos - k2 * sin
    k2r = k1 * sin + k2 * cos
    kr_ref[...] = _rope_pair_to_rows(k1r, k2r)

    p = z[:, C_P:C_P + POOL_WIDTH]
    p_ref[...] = p
    bs = []
    for g, w in enumerate(POOL_WINDOWS):
        lo = g * POOL_GROUP
        acc = p[:, lo:lo + POOL_GROUP]
        for k in range(1, w):
            acc = acc + hist_ref[POOL_HIST - k, :, lo:lo + POOL_GROUP]
        pooled = acc * (1.0 / w) - p[:, lo:lo + POOL_GROUP]
        bs.append(jnp.dot(pooled.astype(BF16), wpool_ref[g], preferred_element_type=F32))
    b = jnp.concatenate(bs, axis=1) * pscale_ref[...]
    sga_ref[...] = _sigmoid(z[:, C_GA:C_GA + D_MODEL]).astype(BF16)
    gbb_ref[...] = (_sigmoid(z[:, C_GB:C_GB + D_MODEL]) * b).astype(BF16)


def _sfront(xs, cos_s, sin_s, hist_t, wts):
    n = xs.shape[0]
    out_shape = (
        jax.ShapeDtypeStruct((N_HEADS, n, KV_RANK), BF16),
        jax.ShapeDtypeStruct((n, LANES), F32),
        jax.ShapeDtypeStruct((n, LANES), F32),
        jax.ShapeDtypeStruct((n, KV_RANK), F32),
        jax.ShapeDtypeStruct((n, ROPE_DIM), F32),
        jax.ShapeDtypeStruct((n, POOL_WIDTH), F32),
        jax.ShapeDtypeStruct((n, D_MODEL), BF16),
        jax.ShapeDtypeStruct((n, D_MODEL), BF16),
    )
    return pl.pallas_call(
        _sfront_kernel,
        out_shape=out_shape,
        compiler_params=pltpu.CompilerParams(vmem_limit_bytes=VMEM_LIMIT),
        name="sfront",
    )(xs, wts["g1"], wts["w_in"], wts["qg"], wts["w_uq"], wts["kvg"], cos_s, sin_s, wts["w_pool"],
      wts["pool_scale"], hist_t, wts["w_ukp"])


def _sattn_kernel(pt_ref, qlat_ref, qexp_ref, qrope_ref, ckvn_ref, krn_ref, cckv_hbm, ckrt_hbm,
                  o_ref, kbuf, rbuf, sem, m_ref, l_ref, acc_ref, *, pc, nc):
    b = pl.program_id(0)
    c = pl.program_id(1)
    nb = pl.num_programs(0)
    step = b * nc + c
    slot = step % 2
    nt_dims = (((1,), (1,)), ((), ()))

    def start_pages(bb, cc, sl):
        base = bb * (nc * pc) + cc * pc

        def body(pg, carry):
            phys = pt_ref[base + pg]
            pltpu.make_async_copy(cckv_hbm.at[phys], kbuf.at[sl, pg], sem.at[0, sl]).start()
            pltpu.make_async_copy(ckrt_hbm.at[phys], rbuf.at[sl, pg], sem.at[1, sl]).start()
            return carry

        lax.fori_loop(0, pc, body, 0)

    @pl.when(step == 0)
    def _():
        start_pages(b, c, slot)

    nxt = step + 1

    @pl.when(nxt < nb * nc)
    def _():
        start_pages(nxt // nc, nxt % nc, 1 - slot)

    ql = qlat_ref[0]

    @pl.when(c == 0)
    def _():
        cn = ckvn_ref[0]
        s0 = (jnp.sum(ql.astype(F32) * cn, axis=1, keepdims=True)
              + jnp.sum(qrope_ref[0] * krn_ref[0], axis=1, keepdims=True))
        m_ref[...] = jnp.broadcast_to(s0, (N_HEADS, LANES))
        l_ref[...] = jnp.ones((N_HEADS, LANES), F32)
        acc_ref[...] = jnp.broadcast_to(cn, (N_HEADS, KV_RANK))

    pltpu.make_async_copy(cckv_hbm.at[pl.ds(0, pc)], kbuf.at[slot], sem.at[0, slot]).wait()
    pltpu.make_async_copy(ckrt_hbm.at[pl.ds(0, pc)], rbuf.at[slot], sem.at[1, slot]).wait()

    kb = kbuf[slot].reshape(pc * PAGE_SIZE, KV_RANK).astype(BF16)
    s_lat = lax.dot_general(ql, kb, nt_dims, preferred_element_type=F32)
    qexp = qexp_ref[0]
    pieces = []
    for g in range(pc // ROPE_GROUP):
        r8 = rbuf[slot, g * ROPE_GROUP:(g + 1) * ROPE_GROUP].reshape(ROPE_GROUP * ROPE_DIM, PAGE_SIZE)
        og = jnp.dot(qexp, r8.astype(BF16), preferred_element_type=F32)
        pieces += [og[pp * N_HEADS:(pp + 1) * N_HEADS] for pp in range(ROPE_GROUP)]
    s = s_lat + jnp.concatenate(pieces, axis=1)
    m_prev = m_ref[...]
    m_new = jnp.maximum(m_prev, jnp.max(s, axis=1, keepdims=True))
    alpha = jnp.exp2(m_prev - m_new)
    p = jnp.exp2(s - m_new[:, 0:1])
    l_ref[...] = alpha * l_ref[...] + jnp.sum(p, axis=1, keepdims=True)
    m_ref[...] = m_new
    pv = jnp.dot(p.astype(BF16), kb, preferred_element_type=F32)
    acc_ref[...] = acc_ref[...] * jnp.concatenate([alpha] * (KV_RANK // LANES), axis=1) + pv

    @pl.when(c == nc - 1)
    def _():
        inv = 1.0 / l_ref[...]
        o_ref[0] = acc_ref[...] * jnp.concatenate([inv] * (KV_RANK // LANES), axis=1)


def _sattn(page_table, qlat, qexp, qrope, ckv_new, kr_new, cache_ckv, cache_krt, pc):
    n, n_pages = page_table.shape
    nc = n_pages // pc
    grid_spec = pltpu.PrefetchScalarGridSpec(
        num_scalar_prefetch=1,
        grid=(n, nc),
        in_specs=[
            pl.BlockSpec((1, N_HEADS, KV_RANK), lambda b, c, pt: (b, 0, 0)),
            pl.BlockSpec((1, ROPE_GROUP * N_HEADS, ROPE_GROUP * ROPE_DIM), lambda b, c, pt: (b, 0, 0)),
            pl.BlockSpec((1, N_HEADS, ROPE_DIM), lambda b, c, pt: (b, 0, 0)),
            pl.BlockSpec((1, 1, KV_RANK), lambda b, c, pt: (b, 0, 0)),
            pl.BlockSpec((1, 1, ROPE_DIM), lambda b, c, pt: (b, 0, 0)),
            pl.BlockSpec(memory_space=pl.ANY),
            pl.BlockSpec(memory_space=pl.ANY),
        ],
        out_specs=pl.BlockSpec((1, N_HEADS, KV_RANK), lambda b, c, pt: (b, 0, 0)),
        scratch_shapes=[
            pltpu.VMEM((2, pc, PAGE_SIZE, KV_RANK), F32),
            pltpu.VMEM((2, pc, ROPE_DIM, PAGE_SIZE), F32),
            pltpu.SemaphoreType.DMA((2, 2)),
            pltpu.VMEM((N_HEADS, LANES), F32),
            pltpu.VMEM((N_HEADS, LANES), F32),
            pltpu.VMEM((N_HEADS, KV_RANK), F32),
        ],
    )
    return pl.pallas_call(
        functools.partial(_sattn_kernel, pc=pc, nc=nc),
        out_shape=jax.ShapeDtypeStruct((n, N_HEADS, KV_RANK), F32),
        grid_spec=grid_spec,
        compiler_params=pltpu.CompilerParams(
            dimension_semantics=("arbitrary", "arbitrary"), vmem_limit_bytes=VMEM_LIMIT),
        name="sattn",
    )(page_table.reshape(-1), qlat, qexp, qrope, ckv_new, kr_new, cache_ckv, cache_krt)


def _smerge_kernel(o_ref, wuv_ref, sga_ref, gbb_ref, out_ref):
    a = jnp.concatenate(
        [jnp.dot(o_ref[h].astype(BF16), wuv_ref[h], preferred_element_type=F32)
         for h in range(N_HEADS)], axis=1)
    out_ref[...] = (sga_ref[...].astype(F32) * a + gbb_ref[...].astype(F32)).astype(BF16)


def _smerge(o_t, sga, gbb, wts):
    n = sga.shape[0]
    return pl.pallas_call(
        _smerge_kernel,
        out_shape=jax.ShapeDtypeStruct((n, D_MODEL), BF16),
        name="smerge",
    )(o_t, wts["w_uv"], sga, gbb)


def _post_kernel(x_ref, mg_ref, wout_ref, g2_ref, wr_ref, br_ref, x1_ref, h2_ref, ti_ref, tg_ref):
    tm = x_ref.shape[0]
    x1 = x_ref[...] + jnp.dot(mg_ref[...], wout_ref[...], preferred_element_type=F32)
    x1_ref[...] = x1
    h2 = _rms(x1, g2_ref[...])
    _store_row_tiles(h2_ref, h2)
    hh = h2.astype(BF16)
    hl = (h2 - hh.astype(F32)).astype(BF16)
    o1 = jnp.dot(hh, wr_ref[...], preferred_element_type=F32)
    o2 = jnp.dot(hl, wr_ref[:, 0:LANES], preferred_element_type=F32)
    logits = o1[:, 0:LANES] + o1[:, LANES:] + o2 + br_ref[...]
    lane = lax.broadcasted_iota(jnp.int32, (tm, LANES), 1)
    vals = logits
    tops, idxs = [], []
    for _ in range(TOP_K):
        m = jnp.max(vals, axis=1, keepdims=True)
        idx = jnp.min(jnp.where(vals == m, lane, LANES), axis=1, keepdims=True)
        tops.append(m)
        idxs.append(idx)
        vals = jnp.where(lane == idx, NEG_BIG * 2, vals)
    es = [jnp.exp(v - tops[0]) for v in tops]
    den = es[0] + es[1] + es[2] + es[3]
    ti = jnp.zeros((tm, LANES), jnp.int32)
    tg = jnp.zeros((tm, LANES), F32)
    for k in range(TOP_K):
        ti = jnp.where(lane == k, idxs[k], ti)
        tg = jnp.where(lane == k, es[k] / den, tg)
    ti_ref[...] = ti
    tg_ref[...] = tg


def _post(x, merged, wts, tm):
    T = x.shape[0]
    full = lambda shape: pl.BlockSpec(shape, lambda i: (0,) * len(shape))
    row = lambda w: pl.BlockSpec((tm, w), lambda i: (i, 0))
    return pl.pallas_call(
        _post_kernel,
        out_shape=(
            jax.ShapeDtypeStruct((T, D_MODEL), F32),
            jax.ShapeDtypeStruct((T * ROW_SUB, LANES), F32),
            jax.ShapeDtypeStruct((T, LANES), jnp.int32),
            jax.ShapeDtypeStruct((T, LANES), F32),
        ),
        grid=(T // tm,),
        in_specs=[row(D_MODEL), row(D_MODEL), full((D_MODEL, D_MODEL)), full((1, D_MODEL)),
                  full((D_MODEL, 2 * LANES)), full((1, LANES))],
        out_specs=(row(D_MODEL), pl.BlockSpec((tm * ROW_SUB, LANES), lambda i: (i, 0)),
                   row(LANES), row(LANES)),
        compiler_params=pltpu.CompilerParams(
            dimension_semantics=("arbitrary",), vmem_limit_bytes=VMEM_LIMIT),
        name="post",
    )(x, merged, wts["w_out"], wts["g2"], wts["w_r"], wts["b_r"])


def _dispatch_kernel(plan_ref, dest_ref, hp_ref, hs_ref, xs_hbm, hbuf, sem, psem, *, td, ntp, n_slots):
    i = pl.program_id(0)
    n = pl.num_programs(0)
    slot = i % 2

    def wait_tile(s):
        span = pl.ds(0, td * TOP_K * ROW_SUB)
        pltpu.make_async_copy(xs_hbm.at[span], xs_hbm.at[span], sem.at[s]).wait()

    @pl.when(i >= 2)
    def _():
        wait_tile(slot)

    @pl.when(i < ntp)
    def _():
        hbuf[slot] = hp_ref[...]

    @pl.when(i >= ntp)
    def _():
        hbuf[slot] = hs_ref[...]

    def body(r8, c):
        for rr in range(8):
            r = r8 * 8 + rr
            for k in range(TOP_K):
                d = dest_ref[0, 0, r * TOP_K + k]
                pltpu.make_async_copy(_row_tile(hbuf.at[slot], r), _row_tile(xs_hbm, d),
                                      sem.at[slot]).start(priority=k % 2)
        return c

    lax.fori_loop(0, td // 8, body, 0)

    @pl.when(i == n - 1)
    def _():
        def fill(s, c):
            pltpu.make_async_copy(_row_tile(hbuf.at[slot], 0), _row_tile(xs_hbm, s), psem.at[0]).start()
            return c

        def drain(s, c):
            pltpu.make_async_copy(_row_tile(hbuf.at[slot], 0), _row_tile(xs_hbm, s), psem.at[0]).wait()
            return c

        def per_expert(e, c):
            lo = plan_ref[e]
            hi = plan_ref[N_EXPERTS + e]
            lax.fori_loop(lo, hi, fill, 0)
            lax.fori_loop(lo, hi, drain, 0)
            return c

        lax.fori_loop(0, N_EXPERTS, per_expert, 0)

        def tail_copy(c):
            rows = pl.ds(pl.multiple_of(c * (td * ROW_SUB), td * ROW_SUB), td * ROW_SUB)
            return pltpu.make_async_copy(hbuf.at[slot], xs_hbm.at[rows], psem.at[0])

        def fill_tail(c, cc):
            tail_copy(c).start()
            return cc

        def drain_tail(c, cc):
            tail_copy(c).wait()
            return cc

        first_tail = lax.shift_right_logical(plan_ref[2 * N_EXPERTS - 1], td.bit_length() - 1)
        lax.fori_loop(first_tail, n_slots // td, fill_tail, 0)
        lax.fori_loop(first_tail, n_slots // td, drain_tail, 0)
        wait_tile(slot)

        @pl.when(n >= 2)
        def _():
            wait_tile(1 - slot)


def _dispatch(plan, dest, h2_p, h2_s, cap, td):
    T = dest.shape[0]
    nt = T // td
    ntp = h2_p.shape[0] // (td * ROW_SUB)
    tile_spec = lambda f: pl.BlockSpec((td * ROW_SUB, LANES), f)
    grid_spec = pltpu.PrefetchScalarGridSpec(
        num_scalar_prefetch=1,
        grid=(nt,),
        in_specs=[
            pl.BlockSpec((1, 1, td * TOP_K), lambda i, plan: (i, 0, 0), memory_space=pltpu.SMEM),
            tile_spec(lambda i, plan: (jnp.minimum(i, ntp - 1), 0)),
            tile_spec(lambda i, plan: (jnp.maximum(i - ntp, 0), 0)),
        ],
        out_specs=pl.BlockSpec(memory_space=pl.ANY),
        scratch_shapes=[pltpu.VMEM((2, td * ROW_SUB, LANES), F32), pltpu.SemaphoreType.DMA((2,)),
                        pltpu.SemaphoreType.DMA((1,))],
    )
    return pl.pallas_call(
        functools.partial(_dispatch_kernel, td=td, ntp=ntp, n_slots=cap),
        out_shape=jax.ShapeDtypeStruct((cap * ROW_SUB, LANES), F32),
        grid_spec=grid_spec,
        compiler_params=pltpu.CompilerParams(dimension_semantics=("arbitrary",)),
        name="dispatch",
    )(plan, dest.reshape(nt, 1, td * TOP_K), h2_p, h2_s)


def _experts_kernel(te_ref, tf_ref, tv_ref, ts_ref, xs_ref, wgu_ref, bgu_ref, wdn_ref, bdn_ref, out_ref,
                    wgu_bf, wdn_bf):
    t = pl.program_id(0)

    @pl.when(tf_ref[t] == 1)
    def _():
        wgu_bf[...] = wgu_ref[0].astype(BF16)
        wdn_bf[...] = wdn_ref[0].astype(BF16)

    @pl.when(tv_ref[t] == 1)
    def _():
        x = _load_row_tiles(xs_ref, xs_ref.shape[0] // ROW_SUB).astype(BF16)
        gu = jnp.dot(x, wgu_bf[...], preferred_element_type=F32) + bgu_ref[0]
        gate = jnp.minimum(gu[:, 0:D_FF], SWIGLU_LIMIT)
        up = jnp.clip(gu[:, D_FF:], -SWIGLU_LIMIT, SWIGLU_LIMIT)
        act = gate * _sigmoid(gate * SWIGLU_ALPHA)
        hmid = ((up + 1.0) * act).astype(BF16)
        _store_row_tiles(out_ref, jnp.dot(hmid, wdn_bf[...], preferred_element_type=F32) + bdn_ref[0])

    @pl.when(tv_ref[t] == 0)
    def _():
        out_ref[...] = jnp.zeros_like(out_ref)


def _experts(tile_e, tile_first, tile_valid, tile_src, xs, w_gate_up, b_gate_up, w_down, b_down, tm):
    cap = xs.shape[0] // ROW_SUB
    n_tiles = cap // tm
    grid_spec = pltpu.PrefetchScalarGridSpec(
        num_scalar_prefetch=4,
        grid=(n_tiles,),
        in_specs=[
            pl.BlockSpec((tm * ROW_SUB, LANES), lambda t, te, tf, tv, ts: (ts[t], 0)),
            pl.BlockSpec((1, D_MODEL, 2 * D_FF), lambda t, te, tf, tv, ts: (te[t], 0, 0)),
            pl.BlockSpec((1, 1, 2 * D_FF), lambda t, te, tf, tv, ts: (te[t], 0, 0)),
            pl.BlockSpec((1, D_FF, D_MODEL), lambda t, te, tf, tv, ts: (te[t], 0, 0)),
            pl.BlockSpec((1, 1, D_MODEL), lambda t, te, tf, tv, ts: (te[t], 0, 0)),
        ],
        out_specs=pl.BlockSpec((tm * ROW_SUB, LANES), lambda t, te, tf, tv, ts: (t, 0)),
        scratch_shapes=[pltpu.VMEM((D_MODEL, 2 * D_FF), BF16), pltpu.VMEM((D_FF, D_MODEL), BF16)],
    )
    return pl.pallas_call(
        _experts_kernel,
        out_shape=jax.ShapeDtypeStruct((cap * ROW_SUB, LANES), F32),
        grid_spec=grid_spec,
        compiler_params=pltpu.CompilerParams(
            dimension_semantics=("arbitrary",), vmem_limit_bytes=VMEM_LIMIT),
        name="experts",
    )(tile_e, tile_first, tile_valid, tile_src, xs, w_gate_up, b_gate_up.reshape(N_EXPERTS, 1, 2 * D_FF),
      w_down, b_down.reshape(N_EXPERTS, 1, D_MODEL))


def _combine_kernel(dcur_ref, dnxt_ref, x1_ref, tg_ref, gf_ref, yp_hbm, out_ref, buf, sem, *, tm):
    i = pl.program_id(0)
    n = pl.num_programs(0)
    slot = i % 2

    def start_rows(dref, sl):
        def body(r8, c):
            for rr in range(8):
                r = r8 * 8 + rr
                for k in range(TOP_K):
                    d = dref[0, 0, r * TOP_K + k]
                    pltpu.make_async_copy(_row_tile(yp_hbm, d), _row_tile(buf.at[sl], k * tm + r),
                                          sem.at[sl]).start(priority=k % 2)
            return c

        lax.fori_loop(0, tm // 8, body, 0)

    @pl.when(i == 0)
    def _():
        start_rows(dcur_ref, slot)

    @pl.when(i + 1 < n)
    def _():
        start_rows(dnxt_ref, 1 - slot)

    pltpu.make_async_copy(yp_hbm.at[pl.ds(0, TOP_K * tm * ROW_SUB)], buf.at[slot], sem.at[slot]).wait()
    y = x1_ref[...]
    tg = tg_ref[...]
    for k in range(TOP_K):
        y = y + _load_row_tiles(buf.at[slot], tm, first=k * tm) * tg[:, k:k + 1]
    out_ref[...] = _rms(y, gf_ref[...])


def _combine(dest, x1, tg, gf, yp, tm):
    T = x1.shape[0]
    nt = T // tm
    dest3 = dest.reshape(nt, 1, tm * TOP_K)
    return pl.pallas_call(
        functools.partial(_combine_kernel, tm=tm),
        out_shape=jax.ShapeDtypeStruct((T, D_MODEL), F32),
        grid=(nt,),
        in_specs=[
            pl.BlockSpec((1, 1, tm * TOP_K), lambda i: (i, 0, 0), memory_space=pltpu.SMEM),
            pl.BlockSpec((1, 1, tm * TOP_K), lambda i: (jnp.minimum(i + 1, nt - 1), 0, 0),
                         memory_space=pltpu.SMEM),
            pl.BlockSpec((tm, D_MODEL), lambda i: (i, 0)),
            pl.BlockSpec((tm, LANES), lambda i: (i, 0)),
            pl.BlockSpec((1, D_MODEL), lambda i: (0, 0)),
            pl.BlockSpec(memory_space=pl.ANY),
        ],
        out_specs=pl.BlockSpec((tm, D_MODEL), lambda i: (i, 0)),
        scratch_shapes=[pltpu.VMEM((2, TOP_K * tm * ROW_SUB, LANES), F32), pltpu.SemaphoreType.DMA((2,))],
        compiler_params=pltpu.CompilerParams(
            dimension_semantics=("arbitrary",), vmem_limit_bytes=VMEM_LIMIT),
        name="combine",
    )(dest3, dest3, x1, tg, gf, yp)


def _rope_tables(pos):
    inv = ROPE_BASE ** (-jnp.arange(ROPE_HALF, dtype=F32) / ROPE_HALF)
    ang = pos.astype(F32)[:, None] * inv[None, :]
    return jnp.tile(jnp.cos(ang), (1, N_HEADS)), jnp.tile(jnp.sin(ang), (1, N_HEADS))


def _prep_weights(norm1_g, w_in, q_norm_g, w_uq, kv_norm_g, w_uk, w_uv, w_pool, pool_scale, w_out,
                  norm2_g, w_router, b_router):
    i0 = Q_RANK
    i1 = i0 + KV_RANK
    i2 = i1 + ROPE_DIM
    i3 = i2 + POOL_WIDTH
    w_packed = jnp.concatenate([
        w_in[:, 0:i1],
        jnp.tile(w_in[:, i1:i1 + ROPE_HALF], (1, N_HEADS)),
        jnp.tile(w_in[:, i1 + ROPE_HALF:i2], (1, N_HEADS)),
        w_in[:, i2:],
    ], axis=1).astype(BF16)
    wq = w_uq.reshape(Q_RANK, N_HEADS, QK_DIM)
    wq_packed = jnp.concatenate([
        wq[:, :, 0:NOPE_DIM].reshape(Q_RANK, Q_NOPE_W),
        wq[:, :, NOPE_DIM:NOPE_DIM + ROPE_HALF].reshape(Q_RANK, LANES),
        wq[:, :, NOPE_DIM + ROPE_HALF:].reshape(Q_RANK, LANES),
    ], axis=1).astype(BF16)
    zeros = jnp.zeros((NOPE_DIM, KV_RANK), F32)
    w_ukp = jnp.stack([
        jnp.concatenate([
            jnp.concatenate([w_uk[2 * jp], zeros], axis=1),
            jnp.concatenate([zeros, w_uk[2 * jp + 1]], axis=1)], axis=0)
        for jp in range(N_HEADS // 2)]).astype(BF16)
    wr_hi = w_router.astype(BF16)
    wr_lo = (w_router - wr_hi.astype(F32)).astype(BF16)
    pad = ((0, 0), (0, LANES - N_EXPERTS))
    w_r = jnp.concatenate([jnp.pad(wr_hi, pad), jnp.pad(wr_lo, pad)], axis=1)
    b_r = jnp.concatenate([b_router.astype(F32), jnp.full((LANES - N_EXPERTS,), NEG_BIG, F32)])[None, :]
    return {
        "g1": norm1_g[None, :], "w_in": w_packed, "qg": q_norm_g[None, :], "w_uq": wq_packed,
        "kvg": kv_norm_g[None, :], "w_ukp": w_ukp, "w_uv": w_uv.astype(BF16),
        "w_pool": w_pool.astype(BF16), "pool_scale": pool_scale[None, :], "w_out": w_out.astype(BF16),
        "g2": norm2_g[None, :], "w_r": w_r, "b_r": b_r,
    }


def _route(top_i, tm):
    T = top_i.shape[0]
    n_asg = T * TOP_K
    n_tiles = -(-n_asg // tm) + N_EXPERTS
    e = top_i.reshape(-1)
    onehot = (jnp.arange(N_EXPERTS, dtype=jnp.int32)[:, None] == e[None, :]).astype(jnp.int32)
    csum = jnp.cumsum(onehot, axis=1)
    counts = csum[:, -1]
    padded = (counts + tm - 1) // tm * tm
    pend = jnp.cumsum(padded)
    pstart = pend - padded
    dest = jnp.sum(onehot * (csum - 1 + pstart[:, None]), axis=0)
    tile_start = jnp.arange(n_tiles, dtype=jnp.int32) * tm
    tile_e = jnp.minimum(jnp.sum((pend[None, :] <= tile_start[:, None]).astype(jnp.int32), axis=1), N_EXPERTS - 1)
    n_used = pend[-1] // tm
    tile_valid = (tile_start < pend[-1]).astype(jnp.int32)
    tile_first = jnp.concatenate([jnp.ones((1,), jnp.int32), (tile_e[1:] != tile_e[:-1]).astype(jnp.int32)])
    tile_src = jnp.minimum(jnp.arange(n_tiles, dtype=jnp.int32), n_used - 1)
    plan = jnp.concatenate([pstart + counts, pend]).astype(jnp.int32)
    return dest.reshape(T, TOP_K).astype(jnp.int32), plan, tile_e.astype(jnp.int32), tile_first, tile_valid, tile_src


def kernel(x_prompt, x_sample, cache_ckv, cache_krope, state_pool, page_table, meta_tokens, norm1_g, w_in, q_norm_g, w_uq, kv_norm_g, w_uk, w_uv, w_pool, pool_scale, w_out, norm2_g, w_router, b_router, w_gate_up, b_gate_up, w_down, b_down, final_norm_g):
    B, L, _ = x_prompt.shape
    n = x_sample.shape[0]
    n_pages = page_table.shape[1]
    past_len = n_pages * PAGE_SIZE
    wts = _prep_weights(norm1_g[0], w_in[0], q_norm_g[0], w_uq[0], kv_norm_g[0], w_uk[0], w_uv[0],
                        w_pool[0], pool_scale[0], w_out[0], norm2_g[0], w_router[0], b_router[0])

    cos_m, sin_m = _rope_tables(jnp.arange(N_META))
    _, kcat_m, ckv_m, kr_m, p_m, _, _ = _front(
        meta_tokens[None], cos_m, sin_m, jnp.zeros((16, POOL_WIDTH), F32), wts, N_META)

    cos_p, sin_p = _rope_tables(N_META + jnp.arange(L))
    q, kcat, ckv_p, kr_p, ptail, sga, gbb = _front(x_prompt, cos_p, sin_p, p_m[0], wts, FRONT_TM)
    kmeta = jnp.pad(kcat_m[0], ((0, LANES - N_META), (0, 0)))
    merged_p = _attn(q, kcat, kmeta, sga, gbb, wts, ATTN_T)

    cos_s, sin_s = _rope_tables(jnp.full((n,), past_len))
    hist_t = jnp.transpose(state_pool[0], (1, 0, 2))
    xs = x_sample[:, 0, :]
    qlat_t, q1_s, q2_s, ckv_s, kr_s, p_s, sga_s, gbb_s = _sfront(xs, cos_s, sin_s, hist_t, wts)
    qlat_s = jnp.transpose(qlat_t, (1, 0, 2))
    qrope_s = jnp.concatenate([q1_s.reshape(n, N_HEADS, ROPE_HALF), q2_s.reshape(n, N_HEADS, ROPE_HALF)], axis=2)
    blk_r = jnp.arange(ROPE_GROUP * N_HEADS, dtype=jnp.int32)[:, None] // N_HEADS
    blk_c = jnp.arange(ROPE_GROUP * ROPE_DIM, dtype=jnp.int32)[None, :] // ROPE_DIM
    qexp_s = jnp.where(blk_r == blk_c, jnp.tile(qrope_s, (1, ROPE_GROUP, ROPE_GROUP)), 0.0).astype(BF16)
    o_s = _sattn(page_table, qlat_s, qexp_s, qrope_s, ckv_s[:, None, :], kr_s[:, None, :],
                 cache_ckv[0], jnp.swapaxes(cache_krope[0], 1, 2), min(SATTN_PAGES, n_pages))
    merged_s = _smerge(jnp.transpose(o_s, (1, 0, 2)), sga_s, gbb_s, wts)

    Tp = B * L
    x1_p, h2_p, ti_p, tg_p = _post(x_prompt.reshape(Tp, D_MODEL), merged_p.reshape(Tp, D_MODEL), wts, POST_TM)
    x1_s, h2_s, ti_s, tg_s = _post(xs, merged_s, wts, n)
    top_i = jnp.concatenate([ti_p[:, 0:TOP_K], ti_s[:, 0:TOP_K]], axis=0)
    dest, plan, tile_e, tile_first, tile_valid, tile_src = _route(top_i, MOE_TM)
    cap = tile_e.shape[0] * MOE_TM
    xs_g = _dispatch(plan, dest, h2_p, h2_s, cap, DISPATCH_TD)
    yp = _experts(tile_e, tile_first, tile_valid, tile_src, xs_g, w_gate_up[0], b_gate_up[0], w_down[0],
                  b_down[0], MOE_TM)
    gf = final_norm_g[None, :]
    y_prompt = _combine(dest[:Tp], x1_p, tg_p, gf, yp, POST_TM).reshape(B, L, D_MODEL)
    y_sample = _combine(dest[Tp:], x1_s, tg_s, gf, yp, n).reshape(n, 1, D_MODEL)
    new_ckv_prompt = jnp.concatenate([jnp.broadcast_to(ckv_m, (B, N_META, KV_RANK)), ckv_p], axis=1)[None]
    new_krope_prompt = jnp.concatenate([jnp.broadcast_to(kr_m, (B, N_META, ROPE_DIM)), kr_p], axis=1)[None]
    new_pool_prompt = ptail[:, 16 - POOL_HIST:, :][None]
    new_ckv_sample = ckv_s[None, :, None, :]
    new_krope_sample = kr_s[None, :, None, :]
    new_pool_sample = jnp.concatenate([state_pool[0][:, 1:, :], p_s[:, None, :]], axis=1)[None]
    return (y_prompt, y_sample, new_ckv_prompt, new_krope_prompt, new_pool_prompt,
            new_ckv_sample, new_krope_sample, new_pool_sample)
```

```python
import functools
import math

import jax
import jax.numpy as jnp
from jax import lax
from jax.experimental import pallas as pl
from jax.experimental.pallas import tpu as pltpu

F32 = jnp.float32
BF16 = jnp.bfloat16

D_MODEL = 1024
N_META = 16
N_HEADS = 8
Q_RANK = 256
KV_RANK = 256
NOPE_DIM = 64
ROPE_DIM = 32
ROPE_HALF = ROPE_DIM // 2
QK_DIM = NOPE_DIM + ROPE_DIM
V_DIM = D_MODEL // N_HEADS
ROPE_BASE = 10000.0
ATTN_SCALE = 1.0 / math.sqrt(QK_DIM)
Q_SCALE = ATTN_SCALE * math.log2(math.e)
POOL_WIDTH = D_MODEL // 2
POOL_WINDOWS = (2, 4, 8, 16)
POOL_GROUP = POOL_WIDTH // len(POOL_WINDOWS)
POOL_OUT_GROUP = D_MODEL // len(POOL_WINDOWS)
POOL_HIST = max(POOL_WINDOWS) - 1
N_EXPERTS = 32
TOP_K = 4
D_FF = D_MODEL
SWIGLU_LIMIT = 7.0
SWIGLU_ALPHA = 1.702
EPS = 1e-6
PAGE_SIZE = 128

LANES = 128
ROW_SUB = D_MODEL // LANES
NEG_BIG = -1e30

C_CQ = 0
C_CKV = C_CQ + Q_RANK
C_K1 = C_CKV + KV_RANK
C_K2 = C_K1 + LANES
C_P = C_K2 + LANES
C_GA = C_P + POOL_WIDTH
C_GB = C_GA + D_MODEL
C_END = C_GB + D_MODEL
Q_NOPE_W = N_HEADS * NOPE_DIM
Q_COLS = Q_NOPE_W + 2 * LANES
KCAT_W = KV_RANK + 2 * LANES

FRONT_TM = 512
ATTN_T = 512
POST_TM = 512
MOE_TM = 512
SATTN_PAGES = 128
ROPE_GROUP = 2 * LANES // ROPE_DIM
DISPATCH_TD = 128
VMEM_LIMIT = 56 * 1024 * 1024


def _rms(x, g):
    return x * lax.rsqrt(jnp.mean(x * x, axis=-1, keepdims=True) + EPS) * g


def _sigmoid(x):
    return 1.0 / (1.0 + jnp.exp(-x))


def _store_row_tiles(ref, x):
    n = x.shape[0]
    for s in range(ROW_SUB):
        ref[pl.ds(s, n, stride=ROW_SUB), :] = x[:, s * LANES:(s + 1) * LANES]


def _load_row_tiles(ref, n, first=0):
    return jnp.concatenate(
        [ref[pl.ds(first * ROW_SUB + s, n, stride=ROW_SUB), :] for s in range(ROW_SUB)], axis=1)


def _row_tile(ref, r):
    return ref.at[pl.ds(pl.multiple_of(r * ROW_SUB, ROW_SUB), ROW_SUB)]


def _rope_pair_to_rows(k1r, k2r):
    lane = lax.broadcasted_iota(jnp.int32, k1r.shape, 1)
    return jnp.where(lane < ROPE_HALF, k1r, pltpu.roll(k2r, ROPE_HALF, axis=1))[:, 0:ROPE_DIM]


def _front_kernel(x_ref, g1_ref, w_ref, qg_ref, wuq_ref, kvg_ref, cos_ref, sin_ref, wpool_ref,
                  pscale_ref, pprev_ref,
                  q_ref, kcat_ref, ckv_ref, kr_ref, ptail_ref, sga_ref, gbb_ref,
                  pext_ref, *, tm):
    i = pl.program_id(1)
    x = x_ref[0]
    hb = _rms(x, g1_ref[...]).astype(BF16)
    z = jnp.dot(hb, w_ref[...], preferred_element_type=F32)
    cos = cos_ref[...]
    sin = sin_ref[...]

    qn = _rms(z[:, C_CQ:C_CQ + Q_RANK], qg_ref[...]).astype(BF16)
    q = jnp.dot(qn, wuq_ref[...], preferred_element_type=F32) * Q_SCALE
    q1 = q[:, Q_NOPE_W:Q_NOPE_W + LANES]
    q2 = q[:, Q_NOPE_W + LANES:]
    q_ref[0, :, 0:Q_NOPE_W] = q[:, 0:Q_NOPE_W].astype(BF16)
    q_ref[0, :, Q_NOPE_W:Q_NOPE_W + LANES] = (q1 * cos - q2 * sin).astype(BF16)
    q_ref[0, :, Q_NOPE_W + LANES:] = (q1 * sin + q2 * cos).astype(BF16)

    ckv = _rms(z[:, C_CKV:C_CKV + KV_RANK], kvg_ref[...])
    ckv_ref[0] = ckv
    k1 = z[:, C_K1:C_K1 + LANES]
    k2 = z[:, C_K2:C_K2 + LANES]
    k1r = k1 * cos - k2 * sin
    k2r = k1 * sin + k2 * cos
    kcat_ref[0, :, 0:KV_RANK] = ckv.astype(BF16)
    kcat_ref[0, :, KV_RANK:KV_RANK + LANES] = k1r.astype(BF16)
    kcat_ref[0, :, KV_RANK + LANES:] = k2r.astype(BF16)
    kr_ref[0] = _rope_pair_to_rows(k1r, k2r)

    p = z[:, C_P:C_P + POOL_WIDTH]

    @pl.when(i == 0)
    def _():
        pext_ref[0:16, :] = pprev_ref[...]

    @pl.when(i > 0)
    def _():
        pext_ref[0:16, :] = pext_ref[tm:tm + 16, :]

    pext_ref[16:16 + tm, :] = p
    ptail_ref[0] = p[tm - 16:tm, :]
    bs = []
    for g, w in enumerate(POOL_WINDOWS):
        lo = g * POOL_GROUP
        acc = pext_ref[16:16 + tm, lo:lo + POOL_GROUP]
        for k in range(1, w):
            acc = acc + pext_ref[16 - k:16 - k + tm, lo:lo + POOL_GROUP]
        pooled = acc * (1.0 / w) - p[:, lo:lo + POOL_GROUP]
        bs.append(jnp.dot(pooled.astype(BF16), wpool_ref[g], preferred_element_type=F32))
    b = jnp.concatenate(bs, axis=1) * pscale_ref[...]
    sga_ref[0] = _sigmoid(z[:, C_GA:C_GA + D_MODEL]).astype(BF16)
    gbb_ref[0] = (_sigmoid(z[:, C_GB:C_GB + D_MODEL]) * b).astype(BF16)


def _front(x, cos_t, sin_t, pprev, wts, tm):
    B, L, _ = x.shape
    nt = L // tm
    full = lambda shape: pl.BlockSpec(shape, lambda b, i: (0,) * len(shape))
    row = lambda w: pl.BlockSpec((1, tm, w), lambda b, i: (b, i, 0))
    out_shape = (
        jax.ShapeDtypeStruct((B, L, Q_COLS), BF16),
        jax.ShapeDtypeStruct((B, L, KCAT_W), BF16),
        jax.ShapeDtypeStruct((B, L, KV_RANK), F32),
        jax.ShapeDtypeStruct((B, L, ROPE_DIM), F32),
        jax.ShapeDtypeStruct((B, 16, POOL_WIDTH), F32),
        jax.ShapeDtypeStruct((B, L, D_MODEL), BF16),
        jax.ShapeDtypeStruct((B, L, D_MODEL), BF16),
    )
    return pl.pallas_call(
        functools.partial(_front_kernel, tm=tm),
        out_shape=out_shape,
        grid=(B, nt),
        in_specs=[
            row(D_MODEL),
            full((1, D_MODEL)),
            full((D_MODEL, C_END)),
            full((1, Q_RANK)),
            full((Q_RANK, Q_COLS)),
            full((1, KV_RANK)),
            pl.BlockSpec((tm, LANES), lambda b, i: (i, 0)),
            pl.BlockSpec((tm, LANES), lambda b, i: (i, 0)),
            full((len(POOL_WINDOWS), POOL_GROUP, POOL_OUT_GROUP)),
            full((1, D_MODEL)),
            full((16, POOL_WIDTH)),
        ],
        out_specs=(
            row(Q_COLS), row(KCAT_W), row(KV_RANK), row(ROPE_DIM),
            pl.BlockSpec((1, 16, POOL_WIDTH), lambda b, i: (b, 0, 0)),
            row(D_MODEL), row(D_MODEL),
        ),
        scratch_shapes=[pltpu.VMEM((tm + 16, POOL_WIDTH), F32)],
        compiler_params=pltpu.CompilerParams(
            dimension_semantics=("arbitrary", "arbitrary"), vmem_limit_bytes=VMEM_LIMIT),
        name="front",
    )(x, wts["g1"], wts["w_in"], wts["qg"], wts["w_uq"], wts["kvg"], cos_t, sin_t, wts["w_pool"],
      wts["pool_scale"], pprev)


def _build_qcat(q, wukp_ref, qcat_ref, t):
    lane = lax.broadcasted_iota(jnp.int32, (t, LANES), 1)
    q1 = q[:, Q_NOPE_W:Q_NOPE_W + LANES].astype(F32)
    q2 = q[:, Q_NOPE_W + LANES:].astype(F32)
    zero = jnp.zeros_like(q1)
    for jp in range(N_HEADS // 2):
        ql = jnp.dot(q[:, jp * LANES:(jp + 1) * LANES], wukp_ref[jp], preferred_element_type=F32)
        for s in range(2):
            h = 2 * jp + s
            sel = (lane >= h * ROPE_HALF) & (lane < (h + 1) * ROPE_HALF)
            qcat_ref[h * t:(h + 1) * t, 0:KV_RANK] = ql[:, s * KV_RANK:(s + 1) * KV_RANK].astype(BF16)
            qcat_ref[h * t:(h + 1) * t, KV_RANK:KV_RANK + LANES] = jnp.where(sel, q1, zero).astype(BF16)
            qcat_ref[h * t:(h + 1) * t, KV_RANK + LANES:] = jnp.where(sel, q2, zero).astype(BF16)


def _attn_kernel(qi_ref, kj_ref, q_ref, kcat_ref, kmeta_ref, wukp_ref, wuv_ref, sga_ref, gbb_ref,
                 out_ref, qcat_ref, m_ref, l_ref, acc_ref, *, t):
    s_id = pl.program_id(1)
    i = qi_ref[s_id]
    j = kj_ref[s_id]
    rows = N_HEADS * t
    nt_dims = (((1,), (1,)), ((), ()))

    @pl.when(j == 0)
    def _():
        _build_qcat(q_ref[0], wukp_ref, qcat_ref, t)
        km = kmeta_ref[...]
        s = lax.dot_general(qcat_ref[...], km, nt_dims, preferred_element_type=F32)
        col = lax.broadcasted_iota(jnp.int32, (rows, LANES), 1)
        s = jnp.where(col < N_META, s, NEG_BIG)
        m = jnp.max(s, axis=1, keepdims=True)
        p = jnp.exp2(s - m)
        m_ref[...] = jnp.broadcast_to(m, (rows, LANES))
        l_ref[...] = jnp.broadcast_to(jnp.sum(p, axis=1, keepdims=True), (rows, LANES))
        acc_ref[...] = jnp.dot(p.astype(BF16), km[:, 0:KV_RANK], preferred_element_type=F32)

    def kv_tile_update(on_diagonal):
        k = kcat_ref[0]
        s = lax.dot_general(qcat_ref[...], k, nt_dims, preferred_element_type=F32)
        if on_diagonal:
            r_tok = lax.broadcasted_iota(jnp.int32, (rows, t), 0) & (t - 1)
            col = lax.broadcasted_iota(jnp.int32, (rows, t), 1)
            s = jnp.where(col <= r_tok, s, NEG_BIG)
        m_prev = m_ref[...]
        m_new = jnp.maximum(m_prev, jnp.max(s, axis=1, keepdims=True))
        alpha = jnp.exp2(m_prev - m_new)
        p = jnp.exp2(s - jnp.concatenate([m_new] * (t // LANES), axis=1))
        l_ref[...] = alpha * l_ref[...] + jnp.sum(p, axis=1, keepdims=True)
        m_ref[...] = m_new
        pv = jnp.dot(p.astype(BF16), k[:, 0:KV_RANK], preferred_element_type=F32)
        acc_ref[...] = acc_ref[...] * jnp.concatenate([alpha] * (KV_RANK // LANES), axis=1) + pv

    @pl.when(j < i)
    def _():
        kv_tile_update(False)

    @pl.when(j == i)
    def _():
        kv_tile_update(True)
        inv = 1.0 / l_ref[...]
        o = acc_ref[...] * jnp.concatenate([inv] * (KV_RANK // LANES), axis=1)
        a = jnp.concatenate(
            [jnp.dot(o[h * t:(h + 1) * t].astype(BF16), wuv_ref[h], preferred_element_type=F32)
             for h in range(N_HEADS)], axis=1)
        merged = sga_ref[0].astype(F32) * a + gbb_ref[0].astype(F32)
        out_ref[0] = merged.astype(BF16)


def _attn(q, kcat, kmeta, sga, gbb, wts, t):
    B, L, _ = q.shape
    nq = L // t
    pairs = [(i, j) for i in range(nq) for j in range(i + 1)]
    qi = jnp.asarray([p[0] for p in pairs], jnp.int32)
    kj = jnp.asarray([p[1] for p in pairs], jnp.int32)
    rows = N_HEADS * t
    grid_spec = pltpu.PrefetchScalarGridSpec(
        num_scalar_prefetch=2,
        grid=(B, len(pairs)),
        in_specs=[
            pl.BlockSpec((1, t, Q_COLS), lambda b, s, qi, kj: (b, qi[s], 0)),
            pl.BlockSpec((1, t, KCAT_W), lambda b, s, qi, kj: (b, kj[s], 0)),
            pl.BlockSpec((LANES, KCAT_W), lambda b, s, qi, kj: (0, 0)),
            pl.BlockSpec((N_HEADS // 2, LANES, 2 * KV_RANK), lambda b, s, qi, kj: (0, 0, 0)),
            pl.BlockSpec((N_HEADS, KV_RANK, V_DIM), lambda b, s, qi, kj: (0, 0, 0)),
            pl.BlockSpec((1, t, D_MODEL), lambda b, s, qi, kj: (b, qi[s], 0)),
            pl.BlockSpec((1, t, D_MODEL), lambda b, s, qi, kj: (b, qi[s], 0)),
        ],
        out_specs=pl.BlockSpec((1, t, D_MODEL), lambda b, s, qi, kj: (b, qi[s], 0)),
        scratch_shapes=[
            pltpu.VMEM((rows, KCAT_W), BF16),
            pltpu.VMEM((rows, LANES), F32),
            pltpu.VMEM((rows, LANES), F32),
            pltpu.VMEM((rows, KV_RANK), F32),
        ],
    )
    return pl.pallas_call(
        functools.partial(_attn_kernel, t=t),
        out_shape=jax.ShapeDtypeStruct((B, L, D_MODEL), BF16),
        grid_spec=grid_spec,
        compiler_params=pltpu.CompilerParams(
            dimension_semantics=("arbitrary", "arbitrary"), vmem_limit_bytes=VMEM_LIMIT),
        name="attn",
    )(qi, kj, q, kcat, kmeta, wts["w_ukp"], wts["w_uv"], sga, gbb)


def _sfront_kernel(x_ref, g1_ref, w_ref, qg_ref, wuq_ref, kvg_ref, cos_ref, sin_ref, wpool_ref,
                   pscale_ref, hist_ref, wukp_ref,
                   qlat_ref, q1_ref, q2_ref, ckv_ref, kr_ref, p_ref, sga_ref, gbb_ref):
    n = x_ref.shape[0]
    x = x_ref[...]
    hb = _rms(x, g1_ref[...]).astype(BF16)
    z = jnp.dot(hb, w_ref[...], preferred_element_type=F32)
    cos = cos_ref[...]
    sin = sin_ref[...]

    qn = _rms(z[:, C_CQ:C_CQ + Q_RANK], qg_ref[...]).astype(BF16)
    q = jnp.dot(qn, wuq_ref[...], preferred_element_type=F32) * Q_SCALE
    q1 = q[:, Q_NOPE_W:Q_NOPE_W + LANES]
    q2 = q[:, Q_NOPE_W + LANES:]
    q1_ref[...] = q1 * cos - q2 * sin
    q2_ref[...] = q1 * sin + q2 * cos
    qb = q[:, 0:Q_NOPE_W].astype(BF16)
    for jp in range(N_HEADS // 2):
        ql = jnp.dot(qb[:, jp * LANES:(jp + 1) * LANES], wukp_ref[jp], preferred_element_type=F32)
        qlat_ref[2 * jp] = ql[:, 0:KV_RANK].astype(BF16)
        qlat_ref[2 * jp + 1] = ql[:, KV_RANK:].astype(BF16)

    ckv = _rms(z[:, C_CKV:C_CKV + KV_RANK], kvg_ref[...])
    ckv_ref[...] = ckv
    k1 = z[:, C_K1:C_K1 + LANES]
    k2 = z[:, C_K2:C_K2 + LANES]
    k1r = k1 * cos - k2 * sin
    k2r = k1 * sin + k2 * cos
    kr_ref[...] = _rope_pair_to_rows(k1r, k2r)

    p = z[:, C_P:C_P + POOL_WIDTH]
    p_ref[...] = p
    bs = []
    for g, w in enumerate(POOL_WINDOWS):
        lo = g * POOL_GROUP
        acc = p[:, lo:lo + POOL_GROUP]
        for k in range(1, w):
            acc = acc + hist_ref[POOL_HIST - k, :, lo:lo + POOL_GROUP]
        pooled = acc * (1.0 / w) - p[:, lo:lo + POOL_GROUP]
        bs.append(jnp.dot(pooled.astype(BF16), wpool_ref[g], preferred_element_type=F32))
    b = jnp.concatenate(bs, axis=1) * pscale_ref[...]
    sga_ref[...] = _sigmoid(z[:, C_GA:C_GA + D_MODEL]).astype(BF16)
    gbb_ref[...] = (_sigmoid(z[:, C_GB:C_GB + D_MODEL]) * b).astype(BF16)


def _sfront(xs, cos_s, sin_s, hist_t, wts):
    n = xs.shape[0]
    out_shape = (
        jax.ShapeDtypeStruct((N_HEADS, n, KV_RANK), BF16),
        jax.ShapeDtypeStruct((n, LANES), F32),
        jax.ShapeDtypeStruct((n, LANES), F32),
        jax.ShapeDtypeStruct((n, KV_RANK), F32),
        jax.ShapeDtypeStruct((n, ROPE_DIM), F32),
        jax.ShapeDtypeStruct((n, POOL_WIDTH), F32),
        jax.ShapeDtypeStruct((n, D_MODEL), BF16),
        jax.ShapeDtypeStruct((n, D_MODEL), BF16),
    )
    return pl.pallas_call(
        _sfront_kernel,
        out_shape=out_shape,
        compiler_params=pltpu.CompilerParams(vmem_limit_bytes=VMEM_LIMIT),
        name="sfront",
    )(xs, wts["g1"], wts["w_in"], wts["qg"], wts["w_uq"], wts["kvg"], cos_s, sin_s, wts["w_pool"],
      wts["pool_scale"], hist_t, wts["w_ukp"])


def _sattn_kernel(pt_ref, qlat_ref, qexp_ref, qrope_ref, ckvn_ref, krn_ref, cckv_hbm, ckrt_hbm,
                  o_ref, kbuf, rbuf, sem, m_ref, l_ref, acc_ref, *, pc, nc):
    b = pl.program_id(0)
    c = pl.program_id(1)
    nb = pl.num_programs(0)
    step = b * nc + c
    slot = step % 2
    nt_dims = (((1,), (1,)), ((), ()))

    def start_pages(bb, cc, sl):
        base = bb * (nc * pc) + cc * pc

        def body(pg, carry):
            phys = pt_ref[base + pg]
            pltpu.make_async_copy(cckv_hbm.at[phys], kbuf.at[sl, pg], sem.at[0, sl]).start()
            pltpu.make_async_copy(ckrt_hbm.at[phys], rbuf.at[sl, pg], sem.at[1, sl]).start()
            return carry

        lax.fori_loop(0, pc, body, 0)

    @pl.when(step == 0)
    def _():
        start_pages(b, c, slot)

    nxt = step + 1

    @pl.when(nxt < nb * nc)
    def _():
        start_pages(nxt // nc, nxt % nc, 1 - slot)

    ql = qlat_ref[0]

    @pl.when(c == 0)
    def _():
        cn = ckvn_ref[0]
        s0 = (jnp.sum(ql.astype(F32) * cn, axis=1, keepdims=True)
              + jnp.sum(qrope_ref[0] * krn_ref[0], axis=1, keepdims=True))
        m_ref[...] = jnp.broadcast_to(s0, (N_HEADS, LANES))
        l_ref[...] = jnp.ones((N_HEADS, LANES), F32)
        acc_ref[...] = jnp.broadcast_to(cn, (N_HEADS, KV_RANK))

    pltpu.make_async_copy(cckv_hbm.at[pl.ds(0, pc)], kbuf.at[slot], sem.at[0, slot]).wait()
    pltpu.make_async_copy(ckrt_hbm.at[pl.ds(0, pc)], rbuf.at[slot], sem.at[1, slot]).wait()

    kb = kbuf[slot].reshape(pc * PAGE_SIZE, KV_RANK).astype(BF16)
    s_lat = lax.dot_general(ql, kb, nt_dims, preferred_element_type=F32)
    qexp = qexp_ref[0]
    pieces = []
    for g in range(pc // ROPE_GROUP):
        r8 = rbuf[slot, g * ROPE_GROUP:(g + 1) * ROPE_GROUP].reshape(ROPE_GROUP * ROPE_DIM, PAGE_SIZE)
        og = jnp.dot(qexp, r8.astype(BF16), preferred_element_type=F32)
        pieces += [og[pp * N_HEADS:(pp + 1) * N_HEADS] for pp in range(ROPE_GROUP)]
    s = s_lat + jnp.concatenate(pieces, axis=1)
    m_prev = m_ref[...]
    m_new = jnp.maximum(m_prev, jnp.max(s, axis=1, keepdims=True))
    alpha = jnp.exp2(m_prev - m_new)
    p = jnp.exp2(s - m_new[:, 0:1])
    l_ref[...] = alpha * l_ref[...] + jnp.sum(p, axis=1, keepdims=True)
    m_ref[...] = m_new
    pv = jnp.dot(p.astype(BF16), kb, preferred_element_type=F32)
    acc_ref[...] = acc_ref[...] * jnp.concatenate([alpha] * (KV_RANK // LANES), axis=1) + pv

    @pl.when(c == nc - 1)
    def _():
        inv = 1.0 / l_ref[...]
        o_ref[0] = acc_ref[...] * jnp.concatenate([inv] * (KV_RANK // LANES), axis=1)


def _sattn(page_table, qlat, qexp, qrope, ckv_new, kr_new, cache_ckv, cache_krt, pc):
    n, n_pages = page_table.shape
    nc = n_pages // pc
    grid_spec = pltpu.PrefetchScalarGridSpec(
        num_scalar_prefetch=1,
        grid=(n, nc),
        in_specs=[
            pl.BlockSpec((1, N_HEADS, KV_RANK), lambda b, c, pt: (b, 0, 0)),
            pl.BlockSpec((1, ROPE_GROUP * N_HEADS, ROPE_GROUP * ROPE_DIM), lambda b, c, pt: (b, 0, 0)),
            pl.BlockSpec((1, N_HEADS, ROPE_DIM), lambda b, c, pt: (b, 0, 0)),
            pl.BlockSpec((1, 1, KV_RANK), lambda b, c, pt: (b, 0, 0)),
            pl.BlockSpec((1, 1, ROPE_DIM), lambda b, c, pt: (b, 0, 0)),
            pl.BlockSpec(memory_space=pl.ANY),
            pl.BlockSpec(memory_space=pl.ANY),
        ],
        out_specs=pl.BlockSpec((1, N_HEADS, KV_RANK), lambda b, c, pt: (b, 0, 0)),
        scratch_shapes=[
            pltpu.VMEM((2, pc, PAGE_SIZE, KV_RANK), F32),
            pltpu.VMEM((2, pc, ROPE_DIM, PAGE_SIZE), F32),
            pltpu.SemaphoreType.DMA((2, 2)),
            pltpu.VMEM((N_HEADS, LANES), F32),
            pltpu.VMEM((N_HEADS, LANES), F32),
            pltpu.VMEM((N_HEADS, KV_RANK), F32),
        ],
    )
    return pl.pallas_call(
        functools.partial(_sattn_kernel, pc=pc, nc=nc),
        out_shape=jax.ShapeDtypeStruct((n, N_HEADS, KV_RANK), F32),
        grid_spec=grid_spec,
        compiler_params=pltpu.CompilerParams(
            dimension_semantics=("arbitrary", "arbitrary"), vmem_limit_bytes=VMEM_LIMIT),
        name="sattn",
    )(page_table.reshape(-1), qlat, qexp, qrope, ckv_new, kr_new, cache_ckv, cache_krt)


def _smerge_kernel(o_ref, wuv_ref, sga_ref, gbb_ref, out_ref):
    a = jnp.concatenate(
        [jnp.dot(o_ref[h].astype(BF16), wuv_ref[h], preferred_element_type=F32)
         for h in range(N_HEADS)], axis=1)
    out_ref[...] = (sga_ref[...].astype(F32) * a + gbb_ref[...].astype(F32)).astype(BF16)


def _smerge(o_t, sga, gbb, wts):
    n = sga.shape[0]
    return pl.pallas_call(
        _smerge_kernel,
        out_shape=jax.ShapeDtypeStruct((n, D_MODEL), BF16),
        name="smerge",
    )(o_t, wts["w_uv"], sga, gbb)


def _post_kernel(x_ref, mg_ref, wout_ref, g2_ref, wr_ref, br_ref, x1_ref, h2_ref, ti_ref, tg_ref):
    tm = x_ref.shape[0]
    x1 = x_ref[...] + jnp.dot(mg_ref[...], wout_ref[...], preferred_element_type=F32)
    x1_ref[...] = x1
    h2 = _rms(x1, g2_ref[...])
    _store_row_tiles(h2_ref, h2)
    hh = h2.astype(BF16)
    hl = (h2 - hh.astype(F32)).astype(BF16)
    o1 = jnp.dot(hh, wr_ref[...], preferred_element_type=F32)
    o2 = jnp.dot(hl, wr_ref[:, 0:LANES], preferred_element_type=F32)
    logits = o1[:, 0:LANES] + o1[:, LANES:] + o2 + br_ref[...]
    lane = lax.broadcasted_iota(jnp.int32, (tm, LANES), 1)
    vals = logits
    tops, idxs = [], []
    for _ in range(TOP_K):
        m = jnp.max(vals, axis=1, keepdims=True)
        idx = jnp.min(jnp.where(vals == m, lane, LANES), axis=1, keepdims=True)
        tops.append(m)
        idxs.append(idx)
        vals = jnp.where(lane == idx, NEG_BIG * 2, vals)
    es = [jnp.exp(v - tops[0]) for v in tops]
    den = es[0] + es[1] + es[2] + es[3]
    ti = jnp.zeros((tm, LANES), jnp.int32)
    tg = jnp.zeros((tm, LANES), F32)
    for k in range(TOP_K):
        ti = jnp.where(lane == k, idxs[k], ti)
        tg = jnp.where(lane == k, es[k] / den, tg)
    ti_ref[...] = ti
    tg_ref[...] = tg


def _post(x, merged, wts, tm):
    T = x.shape[0]
    full = lambda shape: pl.BlockSpec(shape, lambda i: (0,) * len(shape))
    row = lambda w: pl.BlockSpec((tm, w), lambda i: (i, 0))
    return pl.pallas_call(
        _post_kernel,
        out_shape=(
            jax.ShapeDtypeStruct((T, D_MODEL), F32),
            jax.ShapeDtypeStruct((T * ROW_SUB, LANES), F32),
            jax.ShapeDtypeStruct((T, LANES), jnp.int32),
            jax.ShapeDtypeStruct((T, LANES), F32),
        ),
        grid=(T // tm,),
        in_specs=[row(D_MODEL), row(D_MODEL), full((D_MODEL, D_MODEL)), full((1, D_MODEL)),
                  full((D_MODEL, 2 * LANES)), full((1, LANES))],
        out_specs=(row(D_MODEL), pl.BlockSpec((tm * ROW_SUB, LANES), lambda i: (i, 0)),
                   row(LANES), row(LANES)),
        compiler_params=pltpu.CompilerParams(
            dimension_semantics=("arbitrary",), vmem_limit_bytes=VMEM_LIMIT),
        name="post",
    )(x, merged, wts["w_out"], wts["g2"], wts["w_r"], wts["b_r"])


def _dispatch_kernel(plan_ref, dest_ref, hp_ref, hs_ref, xs_hbm, hbuf, sem, psem, *, td, ntp, n_slots):
    i = pl.program_id(0)
    n = pl.num_programs(0)
    slot = i % 2

    def wait_tile(s):
        span = pl.ds(0, td * TOP_K * ROW_SUB)
        pltpu.make_async_copy(xs_hbm.at[span], xs_hbm.at[span], sem.at[s]).wait()

    @pl.when(i >= 2)
    def _():
        wait_tile(slot)

    @pl.when(i < ntp)
    def _():
        hbuf[slot] = hp_ref[...]

    @pl.when(i >= ntp)
    def _():
        hbuf[slot] = hs_ref[...]

    def body(r8, c):
        for rr in range(8):
            r = r8 * 8 + rr
            for k in range(TOP_K):
                d = dest_ref[0, 0, r * TOP_K + k]
                pltpu.make_async_copy(_row_tile(hbuf.at[slot], r), _row_tile(xs_hbm, d),
                                      sem.at[slot]).start(priority=k % 2)
        return c

    lax.fori_loop(0, td // 8, body, 0)

    @pl.when(i == n - 1)
    def _():
        def fill(s, c):
            pltpu.make_async_copy(_row_tile(hbuf.at[slot], 0), _row_tile(xs_hbm, s), psem.at[0]).start()
            return c

        def drain(s, c):
            pltpu.make_async_copy(_row_tile(hbuf.at[slot], 0), _row_tile(xs_hbm, s), psem.at[0]).wait()
            return c

        def per_expert(e, c):
            lo = plan_ref[e]
            hi = plan_ref[N_EXPERTS + e]
            lax.fori_loop(lo, hi, fill, 0)
            lax.fori_loop(lo, hi, drain, 0)
            return c

        lax.fori_loop(0, N_EXPERTS, per_expert, 0)

        def tail_copy(c):
            rows = pl.ds(pl.multiple_of(c * (td * ROW_SUB), td * ROW_SUB), td * ROW_SUB)
            return pltpu.make_async_copy(hbuf.at[slot], xs_hbm.at[rows], psem.at[0])

        def fill_tail(c, cc):
            tail_copy(c).start()
            return cc

        def drain_tail(c, cc):
            tail_copy(c).wait()
            return cc

        first_tail = lax.shift_right_logical(plan_ref[2 * N_EXPERTS - 1], td.bit_length() - 1)
        lax.fori_loop(first_tail, n_slots // td, fill_tail, 0)
        lax.fori_loop(first_tail, n_slots // td, drain_tail, 0)
        wait_tile(slot)

        @pl.when(n >= 2)
        def _():
            wait_tile(1 - slot)


def _dispatch(plan, dest, h2_p, h2_s, cap, td):
    T = dest.shape[0]
    nt = T // td
    ntp = h2_p.shape[0] // (td * ROW_SUB)
    tile_spec = lambda f: pl.BlockSpec((td * ROW_SUB, LANES), f)
    grid_spec = pltpu.PrefetchScalarGridSpec(
        num_scalar_prefetch=1,
        grid=(nt,),
        in_specs=[
            pl.BlockSpec((1, 1, td * TOP_K), lambda i, plan: (i, 0, 0), memory_space=pltpu.SMEM),
            tile_spec(lambda i, plan: (jnp.minimum(i, ntp - 1), 0)),
            tile_spec(lambda i, plan: (jnp.maximum(i - ntp, 0), 0)),
        ],
        out_specs=pl.BlockSpec(memory_space=pl.ANY),
        scratch_shapes=[pltpu.VMEM((2, td * ROW_SUB, LANES), F32), pltpu.SemaphoreType.DMA((2,)),
                        pltpu.SemaphoreType.DMA((1,))],
    )
    return pl.pallas_call(
        functools.partial(_dispatch_kernel, td=td, ntp=ntp, n_slots=cap),
        out_shape=jax.ShapeDtypeStruct((cap * ROW_SUB, LANES), F32),
        grid_spec=grid_spec,
        compiler_params=pltpu.CompilerParams(dimension_semantics=("arbitrary",)),
        name="dispatch",
    )(plan, dest.reshape(nt, 1, td * TOP_K), h2_p, h2_s)


def _experts_kernel(te_ref, tf_ref, tv_ref, ts_ref, xs_ref, wgu_ref, bgu_ref, wdn_ref, bdn_ref, out_ref,
                    wgu_bf, wdn_bf):
    t = pl.program_id(0)

    @pl.when(tf_ref[t] == 1)
    def _():
        wgu_bf[...] = wgu_ref[0].astype(BF16)
        wdn_bf[...] = wdn_ref[0].astype(BF16)

    @pl.when(tv_ref[t] == 1)
    def _():
        x = _load_row_tiles(xs_ref, xs_ref.shape[0] // ROW_SUB).astype(BF16)
        gu = jnp.dot(x, wgu_bf[...], preferred_element_type=F32) + bgu_ref[0]
        gate = jnp.minimum(gu[:, 0:D_FF], SWIGLU_LIMIT)
        up = jnp.clip(gu[:, D_FF:], -SWIGLU_LIMIT, SWIGLU_LIMIT)
        act = gate * _sigmoid(gate * SWIGLU_ALPHA)
        hmid = ((up + 1.0) * act).astype(BF16)
        _store_row_tiles(out_ref, jnp.dot(hmid, wdn_bf[...], preferred_element_type=F32) + bdn_ref[0])

    @pl.when(tv_ref[t] == 0)
    def _():
        out_ref[...] = jnp.zeros_like(out_ref)


def _experts(tile_e, tile_first, tile_valid, tile_src, xs, w_gate_up, b_gate_up, w_down, b_down, tm):
    cap = xs.shape[0] // ROW_SUB
    n_tiles = cap // tm
    grid_spec = pltpu.PrefetchScalarGridSpec(
        num_scalar_prefetch=4,
        grid=(n_tiles,),
        in_specs=[
            pl.BlockSpec((tm * ROW_SUB, LANES), lambda t, te, tf, tv, ts: (ts[t], 0)),
            pl.BlockSpec((1, D_MODEL, 2 * D_FF), lambda t, te, tf, tv, ts: (te[t], 0, 0)),
            pl.BlockSpec((1, 1, 2 * D_FF), lambda t, te, tf, tv, ts: (te[t], 0, 0)),
            pl.BlockSpec((1, D_FF, D_MODEL), lambda t, te, tf, tv, ts: (te[t], 0, 0)),
            pl.BlockSpec((1, 1, D_MODEL), lambda t, te, tf, tv, ts: (te[t], 0, 0)),
        ],
        out_specs=pl.BlockSpec((tm * ROW_SUB, LANES), lambda t, te, tf, tv, ts: (t, 0)),
        scratch_shapes=[pltpu.VMEM((D_MODEL, 2 * D_FF), BF16), pltpu.VMEM((D_FF, D_MODEL), BF16)],
    )
    return pl.pallas_call(
        _experts_kernel,
        out_shape=jax.ShapeDtypeStruct((cap * ROW_SUB, LANES), F32),
        grid_spec=grid_spec,
        compiler_params=pltpu.CompilerParams(
            dimension_semantics=("arbitrary",), vmem_limit_bytes=VMEM_LIMIT),
        name="experts",
    )(tile_e, tile_first, tile_valid, tile_src, xs, w_gate_up, b_gate_up.reshape(N_EXPERTS, 1, 2 * D_FF),
      w_down, b_down.reshape(N_EXPERTS, 1, D_MODEL))


def _combine_kernel(dcur_ref, dnxt_ref, x1_ref, tg_ref, gf_ref, yp_hbm, out_ref, buf, sem, *, tm):
    i = pl.program_id(0)
    n = pl.num_programs(0)
    slot = i % 2

    def start_rows(dref, sl):
        def body(r8, c):
            for rr in range(8):
                r = r8 * 8 + rr
                for k in range(TOP_K):
                    d = dref[0, 0, r * TOP_K + k]
                    pltpu.make_async_copy(_row_tile(yp_hbm, d), _row_tile(buf.at[sl], k * tm + r),
                                          sem.at[sl]).start(priority=k % 2)
            return c

        lax.fori_loop(0, tm // 8, body, 0)

    @pl.when(i == 0)
    def _():
        start_rows(dcur_ref, slot)

    @pl.when(i + 1 < n)
    def _():
        start_rows(dnxt_ref, 1 - slot)

    pltpu.make_async_copy(yp_hbm.at[pl.ds(0, TOP_K * tm * ROW_SUB)], buf.at[slot], sem.at[slot]).wait()
    y = x1_ref[...]
    tg = tg_ref[...]
    for k in range(TOP_K):
        y = y + _load_row_tiles(buf.at[slot], tm, first=k * tm) * tg[:, k:k + 1]
    out_ref[...] = _rms(y, gf_ref[...])


def _combine(dest, x1, tg, gf, yp, tm):
    T = x1.shape[0]
    nt = T // tm
    dest3 = dest.reshape(nt, 1, tm * TOP_K)
    return pl.pallas_call(
        functools.partial(_combine_kernel, tm=tm),
        out_shape=jax.ShapeDtypeStruct((T, D_MODEL), F32),
        grid=(nt,),
        in_specs=[
            pl.BlockSpec((1, 1, tm * TOP_K), lambda i: (i, 0, 0), memory_space=pltpu.SMEM),
            pl.BlockSpec((1, 1, tm * TOP_K), lambda i: (jnp.minimum(i + 1, nt - 1), 0, 0),
                         memory_space=pltpu.SMEM),
            pl.BlockSpec((tm, D_MODEL), lambda i: (i, 0)),
            pl.BlockSpec((tm, LANES), lambda i: (i, 0)),
            pl.BlockSpec((1, D_MODEL), lambda i: (0, 0)),
            pl.BlockSpec(memory_space=pl.ANY),
        ],
        out_specs=pl.BlockSpec((tm, D_MODEL), lambda i: (i, 0)),
        scratch_shapes=[pltpu.VMEM((2, TOP_K * tm * ROW_SUB, LANES), F32), pltpu.SemaphoreType.DMA((2,))],
        compiler_params=pltpu.CompilerParams(
            dimension_semantics=("arbitrary",), vmem_limit_bytes=VMEM_LIMIT),
        name="combine",
    )(dest3, dest3, x1, tg, gf, yp)


def _rope_tables(pos):
    inv = ROPE_BASE ** (-jnp.arange(ROPE_HALF, dtype=F32) / ROPE_HALF)
    ang = pos.astype(F32)[:, None] * inv[None, :]
    return jnp.tile(jnp.cos(ang), (1, N_HEADS)), jnp.tile(jnp.sin(ang), (1, N_HEADS))


def _prep_weights(norm1_g, w_in, q_norm_g, w_uq, kv_norm_g, w_uk, w_uv, w_pool, pool_scale, w_out,
                  norm2_g, w_router, b_router):
    i0 = Q_RANK
    i1 = i0 + KV_RANK
    i2 = i1 + ROPE_DIM
    i3 = i2 + POOL_WIDTH
    w_packed = jnp.concatenate([
        w_in[:, 0:i1],
        jnp.tile(w_in[:, i1:i1 + ROPE_HALF], (1, N_HEADS)),
        jnp.tile(w_in[:, i1 + ROPE_HALF:i2], (1, N_HEADS)),
        w_in[:, i2:],
    ], axis=1).astype(BF16)
    wq = w_uq.reshape(Q_RANK, N_HEADS, QK_DIM)
    wq_packed = jnp.concatenate([
        wq[:, :, 0:NOPE_DIM].reshape(Q_RANK, Q_NOPE_W),
        wq[:, :, NOPE_DIM:NOPE_DIM + ROPE_HALF].reshape(Q_RANK, LANES),
        wq[:, :, NOPE_DIM + ROPE_HALF:].reshape(Q_RANK, LANES),
    ], axis=1).astype(BF16)
    zeros = jnp.zeros((NOPE_DIM, KV_RANK), F32)
    w_ukp = jnp.stack([
        jnp.concatenate([
            jnp.concatenate([w_uk[2 * jp], zeros], axis=1),
            jnp.concatenate([zeros, w_uk[2 * jp + 1]], axis=1)], axis=0)
        for jp in range(N_HEADS // 2)]).astype(BF16)
    wr_hi = w_router.astype(BF16)
    wr_lo = (w_router - wr_hi.astype(F32)).astype(BF16)
    pad = ((0, 0), (0, LANES - N_EXPERTS))
    w_r = jnp.concatenate([jnp.pad(wr_hi, pad), jnp.pad(wr_lo, pad)], axis=1)
    b_r = jnp.concatenate([b_router.astype(F32), jnp.full((LANES - N_EXPERTS,), NEG_BIG, F32)])[None, :]
    return {
        "g1": norm1_g[None, :], "w_in": w_packed, "qg": q_norm_g[None, :], "w_uq": wq_packed,
        "kvg": kv_norm_g[None, :], "w_ukp": w_ukp, "w_uv": w_uv.astype(BF16),
        "w_pool": w_pool.astype(BF16), "pool_scale": pool_scale[None, :], "w_out": w_out.astype(BF16),
        "g2": norm2_g[None, :], "w_r": w_r, "b_r": b_r,
    }


def _route(top_i, tm):
    T = top_i.shape[0]
    n_asg = T * TOP_K
    n_tiles = -(-n_asg // tm) + N_EXPERTS
    e = top_i.reshape(-1)
    onehot = (jnp.arange(N_EXPERTS, dtype=jnp.int32)[:, None] == e[None, :]).astype(jnp.int32)
    csum = jnp.cumsum(onehot, axis=1)
    counts = csum[:, -1]
    padded = (counts + tm - 1) // tm * tm
    pend = jnp.cumsum(padded)
    pstart = pend - padded
    dest = jnp.sum(onehot * (csum - 1 + pstart[:, None]), axis=0)
    tile_start = jnp.arange(n_tiles, dtype=jnp.int32) * tm
    tile_e = jnp.minimum(jnp.sum((pend[None, :] <= tile_start[:, None]).astype(jnp.int32), axis=1), N_EXPERTS - 1)
    n_used = pend[-1] // tm
    tile_valid = (tile_start < pend[-1]).astype(jnp.int32)
    tile_first = jnp.concatenate([jnp.ones((1,), jnp.int32), (tile_e[1:] != tile_e[:-1]).astype(jnp.int32)])
    tile_src = jnp.minimum(jnp.arange(n_tiles, dtype=jnp.int32), n_used - 1)
    plan = jnp.concatenate([pstart + counts, pend]).astype(jnp.int32)
    return dest.reshape(T, TOP_K).astype(jnp.int32), plan, tile_e.astype(jnp.int32), tile_first, tile_valid, tile_src


def kernel(x_prompt, x_sample, cache_ckv, cache_krope, state_pool, page_table, meta_tokens, norm1_g, w_in, q_norm_g, w_uq, kv_norm_g, w_uk, w_uv, w_pool, pool_scale, w_out, norm2_g, w_router, b_router, w_gate_up, b_gate_up, w_down, b_down, final_norm_g):
    B, L, _ = x_prompt.shape
    n = x_sample.shape[0]
    n_pages = page_table.shape[1]
    past_len = n_pages * PAGE_SIZE
    pages_per_step = min(SATTN_PAGES, n_pages)
    assert x_prompt.shape[2] == D_MODEL and x_sample.shape[1:] == (1, D_MODEL), "unsupported activation shapes"
    assert norm1_g.shape[0] == 1 and cache_ckv.shape[0] == 1, "single trunk layer only"
    assert L % FRONT_TM == 0 and L % ATTN_T == 0 and (B * L) % POST_TM == 0, "prompt length must tile evenly"
    assert (B * L) % DISPATCH_TD == 0 and n % DISPATCH_TD == 0 and MOE_TM % DISPATCH_TD == 0, "token tiles"
    assert n_pages % pages_per_step == 0 and pages_per_step % ROPE_GROUP == 0, "cache pages per step"
    assert past_len + 1 >= max(POOL_WINDOWS) and N_META >= POOL_HIST, "pooling windows must be full"
    wts = _prep_weights(norm1_g[0], w_in[0], q_norm_g[0], w_uq[0], kv_norm_g[0], w_uk[0], w_uv[0],
                        w_pool[0], pool_scale[0], w_out[0], norm2_g[0], w_router[0], b_router[0])

    cos_m, sin_m = _rope_tables(jnp.arange(N_META))
    _, kcat_m, ckv_m, kr_m, p_m, _, _ = _front(
        meta_tokens[None], cos_m, sin_m, jnp.zeros((16, POOL_WIDTH), F32), wts, N_META)

    cos_p, sin_p = _rope_tables(N_META + jnp.arange(L))
    q, kcat, ckv_p, kr_p, ptail, sga, gbb = _front(x_prompt, cos_p, sin_p, p_m[0], wts, FRONT_TM)
    kmeta = jnp.pad(kcat_m[0], ((0, LANES - N_META), (0, 0)))
    merged_p = _attn(q, kcat, kmeta, sga, gbb, wts, ATTN_T)

    cos_s, sin_s = _rope_tables(jnp.full((n,), past_len))
    hist_t = jnp.transpose(state_pool[0], (1, 0, 2))
    xs = x_sample[:, 0, :]
    qlat_t, q1_s, q2_s, ckv_s, kr_s, p_s, sga_s, gbb_s = _sfront(xs, cos_s, sin_s, hist_t, wts)
    qlat_s = jnp.transpose(qlat_t, (1, 0, 2))
    qrope_s = jnp.concatenate([q1_s.reshape(n, N_HEADS, ROPE_HALF), q2_s.reshape(n, N_HEADS, ROPE_HALF)], axis=2)
    blk_r = jnp.arange(ROPE_GROUP * N_HEADS, dtype=jnp.int32)[:, None] // N_HEADS
    blk_c = jnp.arange(ROPE_GROUP * ROPE_DIM, dtype=jnp.int32)[None, :] // ROPE_DIM
    qexp_s = jnp.where(blk_r == blk_c, jnp.tile(qrope_s, (1, ROPE_GROUP, ROPE_GROUP)), 0.0).astype(BF16)
    o_s = _sattn(page_table, qlat_s, qexp_s, qrope_s, ckv_s[:, None, :], kr_s[:, None, :],
                 cache_ckv[0], jnp.swapaxes(cache_krope[0], 1, 2), pages_per_step)
    merged_s = _smerge(jnp.transpose(o_s, (1, 0, 2)), sga_s, gbb_s, wts)

    Tp = B * L
    x1_p, h2_p, ti_p, tg_p = _post(x_prompt.reshape(Tp, D_MODEL), merged_p.reshape(Tp, D_MODEL), wts, POST_TM)
    x1_s, h2_s, ti_s, tg_s = _post(xs, merged_s, wts, n)
    top_i = jnp.concatenate([ti_p[:, 0:TOP_K], ti_s[:, 0:TOP_K]], axis=0)
    dest, plan, tile_e, tile_first, tile_valid, tile_src = _route(top_i, MOE_TM)
    cap = tile_e.shape[0] * MOE_TM
    xs_g = _dispatch(plan, dest, h2_p, h2_s, cap, DISPATCH_TD)
    yp = _experts(tile_e, tile_first, tile_valid, tile_src, xs_g, w_gate_up[0], b_gate_up[0], w_down[0],
                  b_down[0], MOE_TM)
    gf = final_norm_g[None, :]
    y_prompt = _combine(dest[:Tp], x1_p, tg_p, gf, yp, POST_TM).reshape(B, L, D_MODEL)
    y_sample = _combine(dest[Tp:], x1_s, tg_s, gf, yp, n).reshape(n, 1, D_MODEL)
    new_ckv_prompt = jnp.concatenate([jnp.broadcast_to(ckv_m, (B, N_META, KV_RANK)), ckv_p], axis=1)[None]
    new_krope_prompt = jnp.concatenate([jnp.broadcast_to(kr_m, (B, N_META, ROPE_DIM)), kr_p], axis=1)[None]
    new_pool_prompt = ptail[:, 16 - POOL_HIST:, :][None]
    new_ckv_sample = ckv_s[None, :, None, :]
    new_krope_sample = kr_s[None, :, None, :]
    new_pool_sample = jnp.concatenate([state_pool[0][:, 1:, :], p_s[:, None, :]], axis=1)[None]
    return (y_prompt, y_sample, new_ckv_prompt, new_krope_prompt, new_pool_prompt,
            new_ckv_sample, new_krope_sample, new_pool_sample)
```
